```python
import jax
import jax.numpy as jnp
from jax import lax
import numpy as np

D_MODEL = 2048
BATCH = 8
SEQ = 2048
DEPTH = 1

RWKV_WIDTH = 1024
RWKV_HEAD_DIM = 64
RWKV_HEADS = RWKV_WIDTH // RWKV_HEAD_DIM
DECAY_LORA = 64
ICLR_LORA = 64
GN_EPS = 64e-5

MOBA_WIDTH = 1024
MOBA_HEAD_DIM = 64
MOBA_HEADS = MOBA_WIDTH // MOBA_HEAD_DIM
MOBA_BLOCK = 256
MOBA_TOPK = 3
QUERY_CHUNK = 16

RMS_EPS = 1e-6
NEG_INF = -1e30

IN_SPLITS = (RWKV_WIDTH, RWKV_WIDTH, RWKV_WIDTH, RWKV_WIDTH, DECAY_LORA, ICLR_LORA,
             MOBA_WIDTH, MOBA_WIDTH, MOBA_WIDTH, MOBA_WIDTH, D_MODEL, D_MODEL)
IN_COLS = 4 * RWKV_WIDTH + DECAY_LORA + ICLR_LORA + 4 * MOBA_WIDTH + 2 * D_MODEL

kernel_name = "hybrid_rwkv7_moba_gated_block"


def rms_norm(x, g):
    xf = x.astype(jnp.float32)
    xf = xf * lax.rsqrt(jnp.mean(xf * xf, axis=-1, keepdims=True) + RMS_EPS)
    return (xf * g.astype(jnp.float32)).astype(x.dtype)


def token_shift(p, mu):
    prev = jnp.pad(p, ((0, 0), (1, 0), (0, 0)))[:, :-1]
    return p + (prev - p) * mu


def rwkv7_mix(p_r, p_k, p_v, p_wd, p_ad, mu_r, mu_k, mu_v, mu_w, mu_a, w0, w_decay_up,
              a0, w_iclr_up, k_k, k_a, r_k, gn_w, gn_b):
    B, S, _ = p_r.shape
    H, N = RWKV_HEADS, RWKV_HEAD_DIM
    f32 = jnp.float32
    r = token_shift(p_r, mu_r)
    k = token_shift(p_k, mu_k)
    v = token_shift(p_v, mu_v)
    xw = token_shift(p_wd, mu_w)
    xa = token_shift(p_ad, mu_a)
    w = -jax.nn.softplus(-(w0 + jnp.tanh(xw) @ w_decay_up)) - 0.5
    decay = jnp.exp(-jnp.exp(w.astype(f32)))
    a = jax.nn.sigmoid(a0 + xa @ w_iclr_up)
    kk = (k * k_k).reshape(B, S, H, N).astype(f32)
    kk = kk / jnp.maximum(jnp.sqrt(jnp.sum(kk * kk, axis=-1, keepdims=True)), 1e-12)
    k = k * (1.0 + (a - 1.0) * k_a)

    def heads(t):
        return t.reshape(B, S, H, N).astype(f32)

    rh, kh, vh, wh, ah = heads(r), heads(k), heads(v), heads(decay), heads(a)
    bh = kk * ah
    xs = tuple(jnp.moveaxis(t, 1, 0) for t in (rh, wh, kh, vh, kk, bh))

    def step(state, inp):
        r_t, w_t, k_t, v_t, kk_t, b_t = inp
        sa = jnp.einsum('bhvk,bhk->bhv', state, -kk_t)
        state = (state * w_t[:, :, None, :]
                 + sa[..., None] * b_t[:, :, None, :]
                 + v_t[..., None] * k_t[:, :, None, :])
        y_t = jnp.einsum('bhvk,bhk->bhv', state, r_t)
        return state, y_t

    state0 = jnp.zeros((B, H, N, N), f32)
    _, y = lax.scan(step, state0, xs)
    y = jnp.moveaxis(y, 0, 1)
    mean = jnp.mean(y, axis=-1, keepdims=True)
    var = jnp.mean(jnp.square(y - mean), axis=-1, keepdims=True)
    y = ((y - mean) * lax.rsqrt(var + GN_EPS)).reshape(B, S, H * N)
    y = y * gn_w.astype(f32) + gn_b.astype(f32)
    bonus = jnp.sum(rh * kh * r_k.reshape(H, N).astype(f32), axis=-1, keepdims=True) * vh
    y = y + bonus.reshape(B, S, H * N)
    return y.astype(p_r.dtype)


def moba_attention(p_q, p_k, p_v, q_norm_w, k_norm_w):
    B, S, _ = p_q.shape
    H, Dh, BLK, C = MOBA_HEADS, MOBA_HEAD_DIM, MOBA_BLOCK, QUERY_CHUNK
    nb = -(-S // BLK)
    s_pad = nb * BLK
    n_sel = min(MOBA_TOPK, nb)
    scale = Dh ** -0.5
    q = rms_norm(p_q.reshape(B, S, H, Dh), q_norm_w).transpose(0, 2, 1, 3)
    k = rms_norm(p_k.reshape(B, S, H, Dh), k_norm_w).transpose(0, 2, 1, 3)
    v = p_v.reshape(B, S, H, Dh).transpose(0, 2, 1, 3)
    pad = ((0, 0), (0, 0), (0, s_pad - S), (0, 0))
    kb = jnp.pad(k, pad).reshape(B, H, nb, BLK, Dh)
    vb = jnp.pad(v, pad).reshape(B, H, nb, BLK, Dh)

    k_mean = jnp.mean(kb, axis=3)
    q_blk = jnp.arange(S) // BLK
    gate = jnp.einsum('bhsd,bhnd->bhsn', q, k_mean).astype(jnp.float32)
    past = jnp.arange(nb)[None, :] < q_blk[:, None]
    gate = jnp.where(past, gate, NEG_INF)
    _, sel = lax.top_k(gate, n_sel)
    sel_valid = jnp.arange(n_sel)[None, :] < q_blk[:, None]

    b_ix = jnp.arange(B)[:, None, None, None]
    h_ix = jnp.arange(H)[None, :, None, None]
    key_off = jnp.arange(BLK)

    def chunk(c):
        t0 = c * C
        blk = t0 // BLK
        q_c = lax.dynamic_slice_in_dim(q, t0, C, axis=2)
        sel_c = lax.dynamic_slice_in_dim(sel, t0, C, axis=2)
        valid_c = lax.dynamic_slice_in_dim(sel_valid, t0, C, axis=0)
        k_own = lax.dynamic_index_in_dim(kb, blk, axis=2, keepdims=False)
        v_own = lax.dynamic_index_in_dim(vb, blk, axis=2, keepdims=False)
        s_own = jnp.einsum('bhcd,bhjd->bhcj', q_c, k_own).astype(jnp.float32) * scale
        q_pos = t0 + jnp.arange(C)
        k_pos = blk * BLK + key_off
        s_own = jnp.where(k_pos[None, :] <= q_pos[:, None], s_own, NEG_INF)
        k_g = kb[b_ix, h_ix, sel_c]
        v_g = vb[b_ix, h_ix, sel_c]
        s_g = jnp.einsum('bhcd,bhckjd->bhckj', q_c, k_g).astype(jnp.float32) * scale
        s_g = jnp.where(valid_c[:, :, None], s_g, NEG_INF)
        logits = jnp.concatenate([s_own, s_g.reshape(B, H, C, n_sel * BLK)], axis=-1)
        probs = jax.nn.softmax(logits, axis=-1).astype(v.dtype)
        p_own = probs[..., :BLK]
        p_g = probs[..., BLK:].reshape(B, H, C, n_sel, BLK)
        return (jnp.einsum('bhcj,bhjd->bhcd', p_own, v_own)
                + jnp.einsum('bhckj,bhckjd->bhcd', p_g, v_g))

    o = lax.map(chunk, jnp.arange(S // C))
    return o.transpose(1, 0, 3, 2, 4).reshape(B, S, H * Dh)


def setup_inputs(seed: int = 0) -> dict:
    key = jax.random.key(seed)
    ks = jax.random.split(key, 24)
    f32 = jnp.float32
    L = DEPTH

    def nrm(k, shape, scale):
        return jax.random.normal(k, shape, f32) * scale

    n = jnp.arange(RWKV_WIDTH, dtype=f32) / (RWKV_WIDTH - 1)
    decay_speed = -7.0 + 5.0 * n ** 0.85
    return {
        "x": nrm(ks[0], (BATCH, SEQ, D_MODEL), 1.0),
        "norm_w": 1.0 + nrm(ks[1], (L, D_MODEL), 0.02),
        "w_in": nrm(ks[2], (L, D_MODEL, IN_COLS), D_MODEL ** -0.5),
        "mu_r": jax.random.uniform(ks[3], (L, RWKV_WIDTH), f32, 0.1, 0.9),
        "mu_k": jax.random.uniform(ks[4], (L, RWKV_WIDTH), f32, 0.1, 0.9),
        "mu_v": jax.random.uniform(ks[5], (L, RWKV_WIDTH), f32, 0.1, 0.9),
        "mu_w": jax.random.uniform(ks[6], (L, DECAY_LORA), f32, 0.1, 0.9),
        "mu_a": jax.random.uniform(ks[7], (L, ICLR_LORA), f32, 0.1, 0.9),
        "w0": decay_speed[None, :] + 0.5 + nrm(ks[8], (L, RWKV_WIDTH), 0.05),
        "w_decay_up": nrm(ks[9], (L, DECAY_LORA, RWKV_WIDTH), 0.5 * DECAY_LORA ** -0.5),
        "a0": nrm(ks[10], (L, RWKV_WIDTH), 0.1),
        "w_iclr_up": nrm(ks[11], (L, ICLR_LORA, RWKV_WIDTH), 0.5 * ICLR_LORA ** -0.5),
        "k_k": 0.85 + nrm(ks[12], (L, RWKV_WIDTH), 0.02),
        "k_a": 1.0 + nrm(ks[13], (L, RWKV_WIDTH), 0.02),
        "r_k": nrm(ks[14], (L, RWKV_WIDTH), 0.1),
        "gn_w": 1.0 + nrm(ks[15], (L, RWKV_WIDTH), 0.02),
        "gn_b": nrm(ks[16], (L, RWKV_WIDTH), 0.02),
        "q_norm_w": 1.0 + nrm(ks[17], (L, MOBA_HEAD_DIM), 0.02),
        "k_norm_w": 1.0 + nrm(ks[18], (L, MOBA_HEAD_DIM), 0.02),
        "w_proj_rwkv": nrm(ks[19], (L, RWKV_WIDTH, D_MODEL), RWKV_WIDTH ** -0.5),
        "w_proj_moba": nrm(ks[20], (L, MOBA_WIDTH, D_MODEL), MOBA_WIDTH ** -0.5),
        "w_out": nrm(ks[21], (L, D_MODEL, D_MODEL), D_MODEL ** -0.5),
    }


def reference(x, norm_w, w_in, mu_r, mu_k, mu_v, mu_w, mu_a, w0, w_decay_up, a0, w_iclr_up,
              k_k, k_a, r_k, gn_w, gn_b, q_norm_w, k_norm_w, w_proj_rwkv, w_proj_moba, w_out):
    split_at = [int(i) for i in np.cumsum(IN_SPLITS)[:-1]]
    B, S, _ = x.shape
    for layer in range(DEPTH):
        h = rms_norm(x, norm_w[layer])
        p = h @ w_in[layer]
        (p_r, p_k, p_v, z_a, p_wd, p_ad,
         p_q, p_kq, p_vq, z_b, g_a, g_b) = jnp.split(p, split_at, axis=-1)
        y_a = rwkv7_mix(p_r, p_k, p_v, p_wd, p_ad, mu_r[layer], mu_k[layer], mu_v[layer],
                        mu_w[layer], mu_a[layer], w0[layer], w_decay_up[layer], a0[layer],
                        w_iclr_up[layer], k_k[layer], k_a[layer], r_k[layer], gn_w[layer],
                        gn_b[layer])
        y_a = y_a * jax.nn.silu(z_a)
        y_b = moba_attention(p_q, p_kq, p_vq, q_norm_w[layer], k_norm_w[layer])
        y_b = y_b * jax.nn.silu(z_b)
        merged = (jax.nn.sigmoid(g_a) * (y_a @ w_proj_rwkv[layer])
                  + jax.nn.sigmoid(g_b) * (y_b @ w_proj_moba[layer]))
        x = x + merged @ w_out[layer]
    return x
```

```python
import functools

import jax
import jax.numpy as jnp
from jax import lax
from jax.experimental import pallas as pl
from jax.experimental.pallas import tpu as pltpu

F32 = jnp.float32
BF16 = jnp.bfloat16

LANES = 128
HEAD_DIM = 64
RWKV_WIDTH = 1024
MOBA_WIDTH = 1024
LORA = 64
MOBA_BLOCK = 256
MOBA_TOPK = 3
RMS_EPS = 1e-6
GN_EPS = 64e-5
NEG_INF = -1e30
CHUNK = 64
VMEM_LIMIT = 56 * 1024 * 1024

SLAB_R, SLAB_K, SLAB_V, SLAB_ZA = 0, 8, 16, 24
SLAB_Q, SLAB_KQ, SLAB_VQ, SLAB_ZB = 32, 40, 48, 56
SLAB_G = 64
N_SLABS = 96


def _split2(x):
    hi = x.astype(BF16)
    lo = (x - hi.astype(F32)).astype(BF16)
    return hi, lo


def _split3(x):
    hi = x.astype(BF16)
    r1 = x - hi.astype(F32)
    mid = r1.astype(BF16)
    lo = (r1 - mid.astype(F32)).astype(BF16)
    return hi, mid, lo


def _dot(a, b):
    return jnp.dot(a, b, preferred_element_type=F32)


def _dot_nt(a, b):
    return lax.dot_general(a, b, (((1,), (1,)), ((), ())), preferred_element_type=F32)


def _dot_tn(a, b):
    return lax.dot_general(a, b, (((0,), (0,)), ((), ())), preferred_element_type=F32)


def _dot_x3(a, b):
    ah, al = _split2(a)
    bh, bl = _split2(b)
    return _dot(ah, bh) + _dot(al, bh) + _dot(ah, bl)


def _lane_iota(shape):
    return lax.broadcasted_iota(jnp.int32, shape, len(shape) - 1)


def _row_iota(shape):
    return lax.broadcasted_iota(jnp.int32, shape, 0)


def _head_ones():
    r = _row_iota((LANES, LANES))
    c = _lane_iota((LANES, LANES))
    return jnp.where((r < HEAD_DIM) == (c < HEAD_DIM), 1.0, 0.0).astype(BF16)


def _head_sum(x, ones_bd):
    hi, mid, lo = _split3(x)
    return _dot(hi, ones_bd) + _dot(mid, ones_bd) + _dot(lo, ones_bd)


def _bd(x):
    first = _lane_iota(x.shape) < HEAD_DIM
    zero = jnp.zeros_like(x)
    return jnp.concatenate([jnp.where(first, x, zero), jnp.where(first, zero, x)], axis=0)


def _fold(m):
    first = _lane_iota((HEAD_DIM, LANES)) < HEAD_DIM
    return jnp.where(first, m[:HEAD_DIM], m[HEAD_DIM:])


def _in_proj_kernel(x_ref, nw_ref, w_ref, wl_ref, p_ref, pl_ref, h_ref, *, slabs):
    @pl.when(pl.program_id(1) == 0)
    def _():
        x = x_ref[...]
        ms = jnp.mean(x * x, axis=-1, keepdims=True)
        h = ((x * lax.rsqrt(ms + RMS_EPS)) * nw_ref[...]).astype(BF16)
        h_ref[...] = h
        pl_ref[...] = _dot(h, wl_ref[...])

    acc = _dot(h_ref[...], w_ref[...])
    for c in range(slabs):
        p_ref[c] = acc[:, c * LANES:(c + 1) * LANES]


def _in_proj(x2, norm_w, w_main, w_lora, *, tm, tn):
    m, d = x2.shape
    n = w_main.shape[1]
    slabs = tn // LANES
    return pl.pallas_call(
        functools.partial(_in_proj_kernel, slabs=slabs),
        grid=(m // tm, n // tn),
        in_specs=[
            pl.BlockSpec((tm, d), lambda i, j: (i, 0)),
            pl.BlockSpec((1, d), lambda i, j: (0, 0)),
            pl.BlockSpec((d, tn), lambda i, j: (0, j)),
            pl.BlockSpec((d, LANES), lambda i, j: (0, 0)),
        ],
        out_specs=[
            pl.BlockSpec((slabs, tm, LANES), lambda i, j: (j, i, 0)),
            pl.BlockSpec((tm, LANES), lambda i, j: (i, 0)),
        ],
        out_shape=[
            jax.ShapeDtypeStruct((n // LANES, m, LANES), F32),
            jax.ShapeDtypeStruct((m, LANES), F32),
        ],
        scratch_shapes=[pltpu.VMEM((tm, d), BF16)],
        compiler_params=pltpu.CompilerParams(
            dimension_semantics=("arbitrary", "arbitrary"), vmem_limit_bytes=VMEM_LIMIT),
        name="in_proj",
    )(x2, norm_w, w_main, w_lora)


def _mm(a, b):
    return _dot(a.astype(BF16), _bd(b).astype(BF16))


def _mm_nt(a, b):
    return _dot_nt(a.astype(BF16), _bd(b).astype(BF16))


def _mm_tn(a, b):
    return _fold(_dot_tn(a.astype(BF16), b.astype(BF16)))


def _rwkv_kernel(r_ref, k_ref, v_ref, z_ref, lo_ref, prm_ref, wup_ref, o_ref, *, seq):
    n_chunks = seq // CHUNK
    L = CHUNK
    prm = prm_ref[...]
    mu_r, mu_k, mu_v = prm[0:1], prm[1:2], prm[2:3]
    w0, a0 = prm[3:4], prm[4:5]
    k_k, k_a, r_k = prm[5:6], prm[6:7], prm[7:8]
    gn_w, gn_b = prm[8:9], prm[9:10]
    mu_l = prm[10:11]
    wup = wup_ref[...]
    wup_hi, wup_lo = _split2(wup)

    ones_bd = _head_ones()
    row = _row_iota((L, LANES))
    lane = _lane_iota((L, LANES))
    col = lane & (HEAD_DIM - 1)
    strict = col < row
    incl = col <= row
    eye = col == row
    first_lanes = lane < HEAD_DIM
    tri = jnp.where(_lane_iota((L, L)) <= _row_iota((L, L)), 1.0, 0.0).astype(BF16)

    def shifted(ref, c, mu):
        start = pl.multiple_of(c * L, L)
        cur = ref[pl.ds(start, L), :]
        prev_start = pl.multiple_of(jnp.maximum(c * L - 8, 0), 8)
        tail = ref[pl.ds(prev_start, 8), :][7:8, :]
        tail = tail * (c > 0).astype(F32)
        prev = jnp.where(row == 0, tail, pltpu.roll(cur, 1, 0))
        return cur + (prev - cur) * mu

    def chunk(c, h0):
        r = shifted(r_ref, c, mu_r)
        k = shifted(k_ref, c, mu_k)
        v = shifted(v_ref, c, mu_v)
        xl = shifted(lo_ref, c, mu_l)
        z = z_ref[pl.ds(pl.multiple_of(c * L, L), L), :]

        feat = jnp.where(first_lanes, jnp.tanh(xl), xl)
        fh, fl = _split2(feat)
        up = _dot(fh, wup_hi) + _dot(fl, wup_hi) + _dot(fh, wup_lo)
        wpre = w0 + up[:, :LANES]
        apre = a0 + up[:, LANES:]
        neg = -wpre
        softplus = jnp.maximum(neg, 0.0) + jnp.log(1.0 + jnp.exp(-jnp.abs(neg)))
        logw = -jnp.exp(-softplus - 0.5)
        a = 1.0 / (1.0 + jnp.exp(-apre))

        kk = k * k_k
        kk = kk / jnp.maximum(jnp.sqrt(_head_sum(kk * kk, ones_bd)), 1e-12)
        k2 = k * (1.0 + (a - 1.0) * k_a)
        b = kk * a
        bonus = _head_sum(r * k2 * r_k, ones_bd) * v

        l1, l2, l3 = _split3(logw)
        cum = _dot(tri, l1) + _dot(tri, l2) + _dot(tri, l3)
        cum_l = cum[L - 1:L, :]
        g_in = jnp.exp(cum)
        g_inv = jnp.exp(-cum)
        g_rest = jnp.exp(cum_l - cum)
        at = -kk * jnp.exp(cum - logw)
        rt = r * g_in
        bt = b * g_inv
        kt = k2 * g_inv
        bh = b * g_rest
        kh = k2 * g_rest

        lhs = jnp.concatenate([at, rt], axis=0)
        ab = _mm_nt(lhs, bt)
        ak = _mm_nt(lhs, kt)
        zero = jnp.zeros((L, LANES), F32)
        n_ab = jnp.where(strict, ab[:L], zero)
        n_ak = jnp.where(strict, ak[:L], zero)
        a_rb = jnp.where(incl, ab[L:], zero)
        a_rk = jnp.where(incl, ak[L:], zero)

        t = jnp.where(eye, 1.0, 0.0) + n_ab
        pw = _mm(n_ab, n_ab)
        steps = CHUNK.bit_length() - 2
        for s in range(steps):
            t = t + _mm(t, pw)
            if s + 1 < steps:
                pw = _mm(pw, pw)

        akv = _mm(n_ak, v)
        p = _mm(t, at)
        q = _mm(t, akv)
        rp = rt + _mm(a_rb, p)
        y0 = _mm(a_rb, q) + _mm(a_rk, v)
        g = jnp.where(eye, jnp.exp(cum_l), 0.0) + _mm_tn(bh, p)
        hadd = _mm_tn(bh, q) + _mm_tn(kh, v)

        y = _mm(rp, h0) + y0
        h_new = _dot_x3(g, _bd(h0)) + hadd

        mean = _head_sum(y, ones_bd) * (1.0 / HEAD_DIM)
        yc = y - mean
        var = _head_sum(yc * yc, ones_bd) * (1.0 / HEAD_DIM)
        yn = yc * lax.rsqrt(var + GN_EPS) * gn_w + gn_b + bonus
        out = yn * (z / (1.0 + jnp.exp(-z)))
        o_ref[pl.ds(pl.multiple_of(c * L, L), L), :] = out.astype(o_ref.dtype)
        return h_new

    lax.fori_loop(0, n_chunks, chunk, jnp.zeros((HEAD_DIM, LANES), F32))


def _rwkv(p3, plora, prm, wup, *, batch, seq):
    hp = RWKV_WIDTH // LANES

    def slab(base):
        return pl.BlockSpec((None, seq, LANES), lambda b, h: (base + h, b, 0))

    return pl.pallas_call(
        functools.partial(_rwkv_kernel, seq=seq),
        grid=(batch, hp),
        in_specs=[
            slab(SLAB_R), slab(SLAB_K), slab(SLAB_V), slab(SLAB_ZA),
            pl.BlockSpec((seq, LANES), lambda b, h: (b, 0)),
            pl.BlockSpec((None, 16, LANES), lambda b, h: (h, 0, 0)),
            pl.BlockSpec((None, LANES, 2 * LANES), lambda b, h: (h, 0, 0)),
        ],
        out_specs=pl.BlockSpec((seq, LANES), lambda b, h: (b, h)),
        out_shape=jax.ShapeDtypeStruct((batch * seq, RWKV_WIDTH), BF16),
        compiler_params=pltpu.CompilerParams(
            dimension_semantics=("arbitrary", "arbitrary"), vmem_limit_bytes=VMEM_LIMIT),
        name="rwkv7_mix",
    )(p3, p3, p3, p3, plora, prm, wup)


def _moba_kernel(q_ref, k_ref, v_ref, z_ref, qw_ref, kw_ref, o_ref, qs_ref, ks_ref, *, seq):
    blk = MOBA_BLOCK
    nb = seq // blk
    ones_bd = _head_ones()

    def normed(ref, w_ref):
        x = ref[...]
        ms = _head_sum(x * x, ones_bd) * (1.0 / HEAD_DIM)
        return x * lax.rsqrt(ms + RMS_EPS) * w_ref[...]

    qn = normed(q_ref, qw_ref)
    kn = normed(k_ref, kw_ref)
    qs_ref[...] = qn
    ks_ref[...] = kn

    km = jnp.mean(kn.reshape(nb, blk, LANES), axis=1)
    first_nb = _lane_iota((nb, LANES)) < HEAD_DIM
    zero_nb = jnp.zeros_like(km)
    km_bd = jnp.concatenate([jnp.where(first_nb, km, zero_nb), jnp.where(first_nb, zero_nb, km)], axis=0)
    km_hi, km_lo = _split2(km_bd)

    first_q = _lane_iota((blk, LANES)) < HEAD_DIM
    glane = _lane_iota((blk, 2 * nb))
    causal = _lane_iota((blk, blk)) <= _row_iota((blk, blk))
    scale = HEAD_DIM ** -0.5

    for i in range(nb):
        rows = slice(i * blk, (i + 1) * blk)
        qi = qs_ref[rows, :]
        n_keys = (i + 1) * blk
        kb = ks_ref[0:n_keys, :].astype(BF16)
        vb = v_ref[0:n_keys, :].astype(BF16)
        if i > MOBA_TOPK:
            q_hi, q_lo = _split2(qi)
            gate = _dot_nt(q_hi, km_hi) + _dot_nt(q_lo, km_hi) + _dot_nt(q_hi, km_lo)
        outs = []
        for h in range(2):
            qh = jnp.where(first_q if h == 0 else ~first_q, qi * scale, 0.0).astype(BF16)
            s = _dot_nt(qh, kb)
            pieces = []
            for j in range(i):
                sj = s[:, j * blk:(j + 1) * blk]
                if i > MOBA_TOPK:
                    gj = gate[:, h * nb + j:h * nb + j + 1]
                    cand = (glane >= h * nb) & (glane < h * nb + i)
                    beats = cand & ((gate > gj) | ((gate == gj) & (glane < h * nb + j)))
                    rank = jnp.sum(jnp.where(beats, 1.0, 0.0), axis=1, keepdims=True)
                    sj = jnp.where(rank < MOBA_TOPK, sj, NEG_INF)
                pieces.append(sj)
            pieces.append(jnp.where(causal, s[:, i * blk:], NEG_INF))
            s = jnp.concatenate(pieces, axis=1) if i > 0 else pieces[0]
            m = jnp.max(s, axis=1, keepdims=True)
            e = jnp.exp(s - m)
            denom = jnp.sum(e, axis=1, keepdims=True)
            prob = (e / denom).astype(BF16)
            outs.append(_dot(prob, vb))
        o = jnp.where(first_q, outs[0], outs[1])
        z = z_ref[rows, :]
        o_ref[rows, :] = (o * (z / (1.0 + jnp.exp(-z)))).astype(o_ref.dtype)


def _moba(p3, qw, kw, *, batch, seq):
    hp = MOBA_WIDTH // LANES

    def slab(base):
        return pl.BlockSpec((None, seq, LANES), lambda b, h: (base + h, b, 0))

    return pl.pallas_call(
        functools.partial(_moba_kernel, seq=seq),
        grid=(batch, hp),
        in_specs=[
            slab(SLAB_Q), slab(SLAB_KQ), slab(SLAB_VQ), slab(SLAB_ZB),
            pl.BlockSpec((1, LANES), lambda b, h: (0, 0)),
            pl.BlockSpec((1, LANES), lambda b, h: (0, 0)),
        ],
        out_specs=pl.BlockSpec((seq, LANES), lambda b, h: (b, h)),
        out_shape=jax.ShapeDtypeStruct((batch * seq, MOBA_WIDTH), BF16),
        scratch_shapes=[pltpu.VMEM((seq, LANES), F32), pltpu.VMEM((seq, LANES), F32)],
        compiler_params=pltpu.CompilerParams(
            dimension_semantics=("arbitrary", "arbitrary"), vmem_limit_bytes=VMEM_LIMIT),
        name="moba_attention",
    )(p3, p3, p3, p3, qw, kw)


def _out_kernel(x_ref, ya_ref, yb_ref, g_ref, wa_ref, wb_ref, wo_ref, o_ref, *, d_model):
    n = d_model // LANES
    pa = _dot(ya_ref[...], wa_ref[...])
    pb = _dot(yb_ref[...], wb_ref[...])
    ga = jnp.concatenate([g_ref[c] for c in range(n)], axis=1)
    gb = jnp.concatenate([g_ref[n + c] for c in range(n)], axis=1)
    merged = pa / (1.0 + jnp.exp(-ga)) + pb / (1.0 + jnp.exp(-gb))
    o_ref[...] = x_ref[...] + _dot(merged.astype(BF16), wo_ref[...])


def _out_proj(x2, ya, yb, p3, wa, wb, wo, *, tm):
    m, d = x2.shape
    n_g = 2 * d // LANES
    const = dict(pipeline_mode=pl.Buffered(1))
    return pl.pallas_call(
        functools.partial(_out_kernel, d_model=d),
        grid=(m // tm,),
        in_specs=[
            pl.BlockSpec((tm, d), lambda i: (i, 0)),
            pl.BlockSpec((tm, RWKV_WIDTH), lambda i: (i, 0)),
            pl.BlockSpec((tm, MOBA_WIDTH), lambda i: (i, 0)),
            pl.BlockSpec((n_g, tm, LANES), lambda i: (SLAB_G // n_g, i, 0)),
            pl.BlockSpec((RWKV_WIDTH, d), lambda i: (0, 0), **const),
            pl.BlockSpec((MOBA_WIDTH, d), lambda i: (0, 0), **const),
            pl.BlockSpec((d, d), lambda i: (0, 0), **const),
        ],
        out_specs=pl.BlockSpec((tm, d), lambda i: (i, 0)),
        out_shape=jax.ShapeDtypeStruct((m, d), F32),
        compiler_params=pltpu.CompilerParams(
            dimension_semantics=("arbitrary",), vmem_limit_bytes=VMEM_LIMIT),
        name="merge_out_proj",
    )(x2, ya, yb, p3, wa, wb, wo)


def _layer(x2, batch, seq, norm_w, w_in, mu_r, mu_k, mu_v, mu_w, mu_a, w0, w_decay_up, a0, w_iclr_up,
           k_k, k_a, r_k, gn_w, gn_b, q_norm_w, k_norm_w, w_proj_rwkv, w_proj_moba, w_out):
    d = x2.shape[1]
    lora_at = 4 * RWKV_WIDTH
    w_main = jnp.concatenate([w_in[:, :lora_at], w_in[:, lora_at + 2 * LORA:]], axis=1).astype(BF16)
    w_lora = w_in[:, lora_at:lora_at + 2 * LORA].astype(BF16)
    assert w_main.shape[1] == N_SLABS * LANES and d == (N_SLABS * LANES - SLAB_G * LANES) // 2

    p3, plora = _in_proj(x2, norm_w.reshape(1, d), w_main, w_lora, tm=min(1024, x2.shape[0]), tn=512)

    hp = RWKV_WIDTH // LANES
    vecs = jnp.stack([mu_r, mu_k, mu_v, w0, a0, k_k, k_a, r_k, gn_w, gn_b]).reshape(10, hp, LANES)
    mu_l = jnp.broadcast_to(jnp.concatenate([mu_w, mu_a]).reshape(1, 1, LANES), (1, hp, LANES))
    prm = jnp.concatenate([vecs, mu_l, jnp.zeros((5, hp, LANES), F32)], axis=0).transpose(1, 0, 2)
    zeros = jnp.zeros((hp, LORA, LANES), F32)
    wd = w_decay_up.reshape(LORA, hp, LANES).transpose(1, 0, 2)
    wa_up = w_iclr_up.reshape(LORA, hp, LANES).transpose(1, 0, 2)
    wup = jnp.concatenate([jnp.concatenate([wd, zeros], axis=2),
                           jnp.concatenate([zeros, wa_up], axis=2)], axis=1)

    ya = _rwkv(p3, plora, prm, wup, batch=batch, seq=seq)
    qw = jnp.tile(q_norm_w, 2).reshape(1, LANES)
    kw = jnp.tile(k_norm_w, 2).reshape(1, LANES)
    yb = _moba(p3, qw, kw, batch=batch, seq=seq)
    return _out_proj(x2, ya, yb, p3, w_proj_rwkv.astype(BF16), w_proj_moba.astype(BF16),
                     w_out.astype(BF16), tm=256)


def kernel(x, norm_w, w_in, mu_r, mu_k, mu_v, mu_w, mu_a, w0, w_decay_up, a0, w_iclr_up, k_k, k_a, r_k,
           gn_w, gn_b, q_norm_w, k_norm_w, w_proj_rwkv, w_proj_moba, w_out):
    batch, seq, d = x.shape
    assert seq % MOBA_BLOCK == 0 and seq % CHUNK == 0
    params = (norm_w, w_in, mu_r, mu_k, mu_v, mu_w, mu_a, w0, w_decay_up, a0, w_iclr_up, k_k, k_a, r_k,
              gn_w, gn_b, q_norm_w, k_norm_w, w_proj_rwkv, w_proj_moba, w_out)
    x2 = x.reshape(batch * seq, d)
    for layer in range(norm_w.shape[0]):
        x2 = _layer(x2, batch, seq, *[p[layer] for p in params])
    return x2.reshape(batch, seq, d)
```

```python
import functools

import jax
import jax.numpy as jnp
from jax import lax
from jax.experimental import pallas as pl
from jax.experimental.pallas import tpu as pltpu

F32 = jnp.float32
BF16 = jnp.bfloat16

LANES = 128
HEAD_DIM = 64
RWKV_WIDTH = 1024
MOBA_WIDTH = 1024
LORA = 64
MOBA_BLOCK = 256
MOBA_TOPK = 3
RMS_EPS = 1e-6
GN_EPS = 64e-5
NEG_INF = -1e30
CHUNK = 64
VMEM_LIMIT = 56 * 1024 * 1024
RWKV_PAIRS_PER_STEP = 2
PREP_UNROLL = 4
SCAN_UNROLL = 2
NORM_ROWS = 256
NORM_UNROLL = 2

SLAB_R, SLAB_K, SLAB_V, SLAB_ZA = 0, 8, 16, 24
SLAB_Q, SLAB_KQ, SLAB_VQ, SLAB_ZB = 32, 40, 48, 56
SLAB_G = 64
N_SLABS = 96


def _split2(x):
    hi = x.astype(BF16)
    lo = (x - hi.astype(F32)).astype(BF16)
    return hi, lo


def _split3(x):
    hi = x.astype(BF16)
    r1 = x - hi.astype(F32)
    mid = r1.astype(BF16)
    lo = (r1 - mid.astype(F32)).astype(BF16)
    return hi, mid, lo


def _dot(a, b):
    return jnp.dot(a, b, preferred_element_type=F32)


def _dot_nt(a, b):
    return lax.dot_general(a, b, (((1,), (1,)), ((), ())), preferred_element_type=F32)


def _dot_tn(a, b):
    return lax.dot_general(a, b, (((0,), (0,)), ((), ())), preferred_element_type=F32)


def _lane_iota(shape):
    return lax.broadcasted_iota(jnp.int32, shape, len(shape) - 1)


def _row_iota(shape):
    return lax.broadcasted_iota(jnp.int32, shape, 0)


def _head_ones():
    r = _row_iota((LANES, LANES))
    c = _lane_iota((LANES, LANES))
    return jnp.where((r < HEAD_DIM) == (c < HEAD_DIM), 1.0, 0.0).astype(BF16)


def _head_sum(x, ones_bd):
    hi, mid, lo = _split3(x)
    return _dot(hi, ones_bd) + _dot(mid, ones_bd) + _dot(lo, ones_bd)


def _bd(x):
    first = _lane_iota(x.shape) < HEAD_DIM
    zero = jnp.zeros_like(x)
    return jnp.concatenate([jnp.where(first, x, zero), jnp.where(first, zero, x)], axis=0)


def _fold(m):
    first = _lane_iota((HEAD_DIM, LANES)) < HEAD_DIM
    return jnp.where(first, m[:HEAD_DIM], m[HEAD_DIM:])


def _in_proj_kernel(x_ref, nw_ref, w_ref, wl_ref, p_ref, pl_ref, h_ref, *, slabs):
    @pl.when(pl.program_id(1) == 0)
    def _():
        x = x_ref[...]
        ms = jnp.mean(x * x, axis=-1, keepdims=True)
        h = ((x * lax.rsqrt(ms + RMS_EPS)) * nw_ref[...]).astype(BF16)
        h_ref[...] = h
        pl_ref[...] = _dot(h, wl_ref[...])

    acc = _dot(h_ref[...], w_ref[...])
    for c in range(slabs):
        p_ref[c] = acc[:, c * LANES:(c + 1) * LANES]


def _in_proj(x2, norm_w, w_main, w_lora, *, tm, tn):
    m, d = x2.shape
    n = w_main.shape[1]
    slabs = tn // LANES
    return pl.pallas_call(
        functools.partial(_in_proj_kernel, slabs=slabs),
        grid=(m // tm, n // tn),
        in_specs=[
            pl.BlockSpec((tm, d), lambda i, j: (i, 0)),
            pl.BlockSpec((1, d), lambda i, j: (0, 0)),
            pl.BlockSpec((d, tn), lambda i, j: (0, j)),
            pl.BlockSpec((d, LANES), lambda i, j: (0, 0)),
        ],
        out_specs=[
            pl.BlockSpec((slabs, tm, LANES), lambda i, j: (j, i, 0)),
            pl.BlockSpec((tm, LANES), lambda i, j: (i, 0)),
        ],
        out_shape=[
            jax.ShapeDtypeStruct((n // LANES, m, LANES), F32),
            jax.ShapeDtypeStruct((m, LANES), F32),
        ],
        scratch_shapes=[pltpu.VMEM((tm, d), BF16)],
        compiler_params=pltpu.CompilerParams(
            dimension_semantics=("arbitrary", "arbitrary"), vmem_limit_bytes=VMEM_LIMIT),
        name="in_proj",
    )(x2, norm_w, w_main, w_lora)


def _mm(a, b):
    return _dot(a.astype(BF16), _bd(b).astype(BF16))


def _mm_nt(a, b):
    return _dot_nt(a.astype(BF16), _bd(b).astype(BF16))


def _mm_tn(a, b):
    return _fold(_dot_tn(a.astype(BF16), b.astype(BF16)))


def _rwkv_kernel(r_ref, k_ref, v_ref, z_ref, lo_ref, prm_ref, wup_ref, o_ref,
                 whi_ref, wlo_ref, hadd_ref, rp_ref, y0_ref, bonus_ref, *, seq, pairs):
    n_chunks = seq // CHUNK
    L = CHUNK
    ones_bd = _head_ones()
    row = _row_iota((L, LANES))
    lane = _lane_iota((L, LANES))
    col = lane & (HEAD_DIM - 1)
    strict = col < row
    incl = col <= row
    eye = col == row
    first_lanes = lane < HEAD_DIM
    tri = jnp.where(_lane_iota((L, L)) <= _row_iota((L, L)), 1.0, 0.0).astype(BF16)

    def shifted(ref, c, mu):
        start = pl.multiple_of(c * L, L)
        cur = ref[pl.ds(start, L), :]
        prev_start = pl.multiple_of(jnp.maximum(c * L - 8, 0), 8)
        tail = ref[pl.ds(prev_start, 8), :][7:8, :]
        tail = tail * (c > 0).astype(F32)
        prev = jnp.where(row == 0, tail, pltpu.roll(cur, 1, 0))
        return cur + (prev - cur) * mu

    def prepare(chains):
        def each(fn, *lists):
            return [fn(*args) for args in zip(*lists)]

        prms = [prm_ref[pr] for _, pr in chains]
        rows = [pl.ds(pl.multiple_of(c * L, L), L) for c, _ in chains]
        r = [shifted(r_ref.at[pr], c, prm[0:1]) for (c, pr), prm in zip(chains, prms)]
        k = [shifted(k_ref.at[pr], c, prm[1:2]) for (c, pr), prm in zip(chains, prms)]
        v = [shifted(v_ref.at[pr], c, prm[2:3]) for (c, pr), prm in zip(chains, prms)]
        xl = [shifted(lo_ref, c, prm[10:11]) for (c, pr), prm in zip(chains, prms)]

        feat = each(lambda x: _split2(jnp.where(first_lanes, jnp.tanh(x), x)), xl)
        wups = [_split2(wup_ref[pr]) for _, pr in chains]
        up = each(lambda f, w: _dot(f[0], w[0]) + _dot(f[1], w[0]) + _dot(f[0], w[1]), feat, wups)

        def decay_log(u, prm):
            neg = -(prm[3:4] + u[:, :LANES])
            softplus = jnp.maximum(neg, 0.0) + jnp.log(1.0 + jnp.exp(-jnp.abs(neg)))
            return -jnp.exp(-softplus - 0.5)

        logw = each(decay_log, up, prms)
        a = each(lambda u, prm: 1.0 / (1.0 + jnp.exp(-(prm[4:5] + u[:, LANES:]))), up, prms)

        kk = each(lambda k_, prm: k_ * prm[5:6], k, prms)
        ksq = each(lambda x: _head_sum(x * x, ones_bd), kk)
        kk = each(lambda x, s: x / jnp.maximum(jnp.sqrt(s), 1e-12), kk, ksq)
        k2 = each(lambda k_, a_, prm: k_ * (1.0 + (a_ - 1.0) * prm[6:7]), k, a, prms)
        b = each(lambda x, a_: x * a_, kk, a)
        rk = each(lambda r_, k_, prm: _head_sum(r_ * k_ * prm[7:8], ones_bd), r, k2, prms)
        for (c, pr), rw, x, v_ in zip(chains, rows, rk, v):
            bonus_ref[pr, rw, :] = x * v_

        lsplit = each(_split3, logw)
        cum = each(lambda s: _dot(tri, s[0]) + _dot(tri, s[1]) + _dot(tri, s[2]), lsplit)
        cum_l = each(lambda x: x[L - 1:L, :], cum)
        g_inv = each(lambda x: jnp.exp(-x), cum)
        g_rest = each(lambda x, xl_: jnp.exp(xl_ - x), cum, cum_l)
        at = each(lambda kk_, x, lw: -kk_ * jnp.exp(x - lw), kk, cum, logw)
        rt = each(lambda r_, x: r_ * jnp.exp(x), r, cum)
        bt = each(lambda b_, g: b_ * g, b, g_inv)
        kt = each(lambda k_, g: k_ * g, k2, g_inv)
        bh = each(lambda b_, g: b_ * g, b, g_rest)
        kh = each(lambda k_, g: k_ * g, k2, g_rest)

        lhs = each(lambda x, y: jnp.concatenate([x, y], axis=0), at, rt)
        ab = each(_mm_nt, lhs, bt)
        ak = each(_mm_nt, lhs, kt)
        zero = jnp.zeros((L, LANES), F32)
        n_ab = each(lambda x: jnp.where(strict, x[:L], zero), ab)
        n_ak = each(lambda x: jnp.where(strict, x[:L], zero), ak)
        a_rb = each(lambda x: jnp.where(incl, x[L:], zero), ab)
        a_rk = each(lambda x: jnp.where(incl, x[L:], zero), ak)

        t = each(lambda x: jnp.where(eye, 1.0, 0.0) + x, n_ab)
        pw = each(_mm, n_ab, n_ab)
        steps = CHUNK.bit_length() - 2
        for s in range(steps):
            t = each(lambda t_, p_: t_ + _mm(t_, p_), t, pw)
            if s + 1 < steps:
                pw = each(_mm, pw, pw)

        akv = each(_mm, n_ak, v)
        p = each(_mm, t, at)
        q = each(_mm, t, akv)
        rp = each(lambda x, m, p_: x + _mm(m, p_), rt, a_rb, p)
        y0 = each(lambda m1, q_, m2, v_: _mm(m1, q_) + _mm(m2, v_), a_rb, q, a_rk, v)
        gt = each(lambda xl_, p_, b_: jnp.where(eye, jnp.exp(xl_), 0.0) + _mm_tn(p_, b_), cum_l, p, bh)
        hadd = each(lambda q_, b_, v_, k_: _mm_tn(q_, b_) + _mm_tn(v_, k_), q, bh, v, kh)
        for i, (c, pr) in enumerate(chains):
            rp_ref[pr, rows[i], :] = rp[i]
            y0_ref[pr, rows[i], :] = y0[i]
            hadd_ref[pr, rows[i], :] = hadd[i]
            w_hi, w_lo = _split2(_bd(gt[i]))
            wrows = pl.ds(pl.multiple_of(c * LANES, LANES), LANES)
            whi_ref[pr, wrows, :] = w_hi
            wlo_ref[pr, wrows, :] = w_lo

    def advance(c, states):
        rows = pl.ds(pl.multiple_of(c * L, L), L)
        wrows = pl.ds(pl.multiple_of(c * LANES, LANES), LANES)
        split = [_split2(s) for s in states]
        new = []
        for pr in range(pairs):
            s_hi, s_lo = split[pr]
            w_hi = whi_ref[pr, wrows, :]
            new.append(_dot(s_hi, w_hi) + _dot(s_lo, w_hi) + _dot(s_hi, wlo_ref[pr, wrows, :])
                       + hadd_ref[pr, rows, :])
        for pr in range(pairs):
            y0_ref[pr, rows, :] = _mm_nt(rp_ref[pr, rows, :], states[pr]) + y0_ref[pr, rows, :]
        return tuple(new)

    def finish(tiles):
        rows = [pl.ds(pl.multiple_of(t * NORM_ROWS, NORM_ROWS), NORM_ROWS) for t, _ in tiles]
        y = [y0_ref[pr, rw, :] for (_, pr), rw in zip(tiles, rows)]
        mean = [_head_sum(x, ones_bd) * (1.0 / HEAD_DIM) for x in y]
        yc = [x - m for x, m in zip(y, mean)]
        var = [_head_sum(x * x, ones_bd) * (1.0 / HEAD_DIM) for x in yc]
        for (_, pr), rw, x, s in zip(tiles, rows, yc, var):
            prm = prm_ref[pr]
            yn = x * lax.rsqrt(s + GN_EPS) * prm[8:9] + prm[9:10] + bonus_ref[pr, rw, :]
            z = z_ref[pr, rw, :]
            o_ref[rw, pr * LANES:(pr + 1) * LANES] = (yn * (z / (1.0 + jnp.exp(-z)))).astype(o_ref.dtype)

    def phase1(i, carry):
        prepare([(i * PREP_UNROLL + u, pr) for u in range(PREP_UNROLL) for pr in range(pairs)])
        return carry

    lax.fori_loop(0, n_chunks // PREP_UNROLL, phase1, 0)

    def phase2(i, states):
        for u in range(SCAN_UNROLL):
            states = advance(i * SCAN_UNROLL + u, states)
        return states

    lax.fori_loop(0, n_chunks // SCAN_UNROLL, phase2,
                  tuple(jnp.zeros((HEAD_DIM, LANES), F32) for _ in range(pairs)))

    def phase3(i, carry):
        finish([(i * NORM_UNROLL + u, pr) for u in range(NORM_UNROLL) for pr in range(pairs)])
        return carry

    lax.fori_loop(0, seq // NORM_ROWS // NORM_UNROLL, phase3, 0)


def _rwkv(p3, plora, prm, wup, *, batch, seq):
    pairs = RWKV_PAIRS_PER_STEP
    groups = RWKV_WIDTH // LANES // pairs
    n_chunks = seq // CHUNK
    assert n_chunks % PREP_UNROLL == 0 and n_chunks % SCAN_UNROLL == 0
    assert seq % (NORM_ROWS * NORM_UNROLL) == 0

    def slab(base):
        return pl.BlockSpec((pairs, seq, LANES), lambda b, h: (base // pairs + h, b, 0))

    def per_pair(rows, dtype):
        return pltpu.VMEM((pairs, rows, LANES), dtype)

    return pl.pallas_call(
        functools.partial(_rwkv_kernel, seq=seq, pairs=pairs),
        grid=(batch, groups),
        in_specs=[
            slab(SLAB_R), slab(SLAB_K), slab(SLAB_V), slab(SLAB_ZA),
            pl.BlockSpec((seq, LANES), lambda b, h: (b, 0)),
            pl.BlockSpec((pairs, 16, LANES), lambda b, h: (h, 0, 0)),
            pl.BlockSpec((pairs, LANES, 2 * LANES), lambda b, h: (h, 0, 0)),
        ],
        out_specs=pl.BlockSpec((seq, pairs * LANES), lambda b, h: (b, h)),
        out_shape=jax.ShapeDtypeStruct((batch * seq, RWKV_WIDTH), BF16),
        scratch_shapes=[
            per_pair(n_chunks * LANES, BF16), per_pair(n_chunks * LANES, BF16),
            per_pair(seq, F32), per_pair(seq, F32), per_pair(seq, F32), per_pair(seq, F32),
        ],
        compiler_params=pltpu.CompilerParams(
            dimension_semantics=("arbitrary", "arbitrary"), vmem_limit_bytes=VMEM_LIMIT),
        name="rwkv7_mix",
    )(p3, p3, p3, p3, plora, prm, wup)


def _moba_kernel(q_ref, k_ref, v_ref, z_ref, qw_ref, kw_ref, o_ref, qs_ref, ks_ref, *, seq):
    blk = MOBA_BLOCK
    nb = seq // blk
    ones_bd = _head_ones()

    def normed(ref, w_ref):
        x = ref[...]
        ms = _head_sum(x * x, ones_bd) * (1.0 / HEAD_DIM)
        return x * lax.rsqrt(ms + RMS_EPS) * w_ref[...]

    qn = normed(q_ref, qw_ref)
    kn = normed(k_ref, kw_ref)
    qs_ref[...] = qn
    ks_ref[...] = kn

    km = jnp.mean(kn.reshape(nb, blk, LANES), axis=1)
    first_nb = _lane_iota((nb, LANES)) < HEAD_DIM
    zero_nb = jnp.zeros_like(km)
    km_bd = jnp.concatenate([jnp.where(first_nb, km, zero_nb), jnp.where(first_nb, zero_nb, km)], axis=0)
    km_hi, km_lo = _split2(km_bd)

    first_q = _lane_iota((blk, LANES)) < HEAD_DIM
    glane = _lane_iota((blk, 2 * nb))
    causal = _lane_iota((blk, blk)) <= _row_iota((blk, blk))
    scale = HEAD_DIM ** -0.5

    for i in range(nb):
        rows = slice(i * blk, (i + 1) * blk)
        qi = qs_ref[rows, :]
        n_keys = (i + 1) * blk
        kb = ks_ref[0:n_keys, :].astype(BF16)
        vb = v_ref[0:n_keys, :].astype(BF16)
        if i > MOBA_TOPK:
            q_hi, q_lo = _split2(qi)
            gate = _dot_nt(q_hi, km_hi) + _dot_nt(q_lo, km_hi) + _dot_nt(q_hi, km_lo)
        outs = []
        for h in range(2):
            qh = jnp.where(first_q if h == 0 else ~first_q, qi * scale, 0.0).astype(BF16)
            s = _dot_nt(qh, kb)
            pieces = []
            for j in range(i):
                sj = s[:, j * blk:(j + 1) * blk]
                if i > MOBA_TOPK:
                    gj = gate[:, h * nb + j:h * nb + j + 1]
                    cand = (glane >= h * nb) & (glane < h * nb + i)
                    beats = cand & ((gate > gj) | ((gate == gj) & (glane < h * nb + j)))
                    rank = jnp.sum(jnp.where(beats, 1.0, 0.0), axis=1, keepdims=True)
                    sj = jnp.where(rank < MOBA_TOPK, sj, NEG_INF)
                pieces.append(sj)
            pieces.append(jnp.where(causal, s[:, i * blk:], NEG_INF))
            s = jnp.concatenate(pieces, axis=1) if i > 0 else pieces[0]
            m = jnp.max(s, axis=1, keepdims=True)
            e = jnp.exp(s - m)
            denom = jnp.sum(e, axis=1, keepdims=True)
            prob = (e / denom).astype(BF16)
            outs.append(_dot(prob, vb))
        o = jnp.where(first_q, outs[0], outs[1])
        z = z_ref[rows, :]
        o_ref[rows, :] = (o * (z / (1.0 + jnp.exp(-z)))).astype(o_ref.dtype)


def _moba(p3, qw, kw, *, batch, seq):
    hp = MOBA_WIDTH // LANES

    def slab(base):
        return pl.BlockSpec((None, seq, LANES), lambda b, h: (base + h, b, 0))

    return pl.pallas_call(
        functools.partial(_moba_kernel, seq=seq),
        grid=(batch, hp),
        in_specs=[
            slab(SLAB_Q), slab(SLAB_KQ), slab(SLAB_VQ), slab(SLAB_ZB),
            pl.BlockSpec((1, LANES), lambda b, h: (0, 0)),
            pl.BlockSpec((1, LANES), lambda b, h: (0, 0)),
        ],
        out_specs=pl.BlockSpec((seq, LANES), lambda b, h: (b, h)),
        out_shape=jax.ShapeDtypeStruct((batch * seq, MOBA_WIDTH), BF16),
        scratch_shapes=[pltpu.VMEM((seq, LANES), F32), pltpu.VMEM((seq, LANES), F32)],
        compiler_params=pltpu.CompilerParams(
            dimension_semantics=("arbitrary", "arbitrary"), vmem_limit_bytes=VMEM_LIMIT),
        name="moba_attention",
    )(p3, p3, p3, p3, qw, kw)


def _out_kernel(x_ref, ya_ref, yb_ref, g_ref, wa_ref, wb_ref, wo_ref, o_ref, *, d_model):
    n = d_model // LANES
    pa = _dot(ya_ref[...], wa_ref[...])
    pb = _dot(yb_ref[...], wb_ref[...])
    ga = jnp.concatenate([g_ref[c] for c in range(n)], axis=1)
    gb = jnp.concatenate([g_ref[n + c] for c in range(n)], axis=1)
    merged = pa / (1.0 + jnp.exp(-ga)) + pb / (1.0 + jnp.exp(-gb))
    o_ref[...] = x_ref[...] + _dot(merged.astype(BF16), wo_ref[...])


def _out_proj(x2, ya, yb, p3, wa, wb, wo, *, tm):
    m, d = x2.shape
    n_g = 2 * d // LANES
    const = dict(pipeline_mode=pl.Buffered(1))
    return pl.pallas_call(
        functools.partial(_out_kernel, d_model=d),
        grid=(m // tm,),
        in_specs=[
            pl.BlockSpec((tm, d), lambda i: (i, 0)),
            pl.BlockSpec((tm, RWKV_WIDTH), lambda i: (i, 0)),
            pl.BlockSpec((tm, MOBA_WIDTH), lambda i: (i, 0)),
            pl.BlockSpec((n_g, tm, LANES), lambda i: (SLAB_G // n_g, i, 0)),
            pl.BlockSpec((RWKV_WIDTH, d), lambda i: (0, 0), **const),
            pl.BlockSpec((MOBA_WIDTH, d), lambda i: (0, 0), **const),
            pl.BlockSpec((d, d), lambda i: (0, 0), **const),
        ],
        out_specs=pl.BlockSpec((tm, d), lambda i: (i, 0)),
        out_shape=jax.ShapeDtypeStruct((m, d), F32),
        compiler_params=pltpu.CompilerParams(
            dimension_semantics=("arbitrary",), vmem_limit_bytes=VMEM_LIMIT),
        name="merge_out_proj",
    )(x2, ya, yb, p3, wa, wb, wo)


def _layer(x2, batch, seq, norm_w, w_in, mu_r, mu_k, mu_v, mu_w, mu_a, w0, w_decay_up, a0, w_iclr_up,
           k_k, k_a, r_k, gn_w, gn_b, q_norm_w, k_norm_w, w_proj_rwkv, w_proj_moba, w_out):
    d = x2.shape[1]
    lora_at = 4 * RWKV_WIDTH
    w_main = jnp.concatenate([w_in[:, :lora_at], w_in[:, lora_at + 2 * LORA:]], axis=1).astype(BF16)
    w_lora = w_in[:, lora_at:lora_at + 2 * LORA].astype(BF16)
    assert w_main.shape[1] == N_SLABS * LANES and d == (N_SLABS * LANES - SLAB_G * LANES) // 2

    p3, plora = _in_proj(x2, norm_w.reshape(1, d), w_main, w_lora, tm=min(1024, x2.shape[0]), tn=512)

    hp = RWKV_WIDTH // LANES
    vecs = jnp.stack([mu_r, mu_k, mu_v, w0, a0, k_k, k_a, r_k, gn_w, gn_b]).reshape(10, hp, LANES)
    mu_l = jnp.broadcast_to(jnp.concatenate([mu_w, mu_a]).reshape(1, 1, LANES), (1, hp, LANES))
    prm = jnp.concatenate([vecs, mu_l, jnp.zeros((5, hp, LANES), F32)], axis=0).transpose(1, 0, 2)
    zeros = jnp.zeros((hp, LORA, LANES), F32)
    wd = w_decay_up.reshape(LORA, hp, LANES).transpose(1, 0, 2)
    wa_up = w_iclr_up.reshape(LORA, hp, LANES).transpose(1, 0, 2)
    wup = jnp.concatenate([jnp.concatenate([wd, zeros], axis=2),
                           jnp.concatenate([zeros, wa_up], axis=2)], axis=1)

    ya = _rwkv(p3, plora, prm, wup, batch=batch, seq=seq)
    qw = jnp.tile(q_norm_w, 2).reshape(1, LANES)
    kw = jnp.tile(k_norm_w, 2).reshape(1, LANES)
    yb = _moba(p3, qw, kw, batch=batch, seq=seq)
    return _out_proj(x2, ya, yb, p3, w_proj_rwkv.astype(BF16), w_proj_moba.astype(BF16),
                     w_out.astype(BF16), tm=256)


def kernel(x, norm_w, w_in, mu_r, mu_k, mu_v, mu_w, mu_a, w0, w_decay_up, a0, w_iclr_up, k_k, k_a, r_k,
           gn_w, gn_b, q_norm_w, k_norm_w, w_proj_rwkv, w_proj_moba, w_out):
    batch, seq, d = x.shape
    assert seq % MOBA_BLOCK == 0 and seq % CHUNK == 0
    params = (norm_w, w_in, mu_r, mu_k, mu_v, mu_w, mu_a, w0, w_decay_up, a0, w_iclr_up, k_k, k_a, r_k,
              gn_w, gn_b, q_norm_w, k_norm_w, w_proj_rwkv, w_proj_moba, w_out)
    x2 = x.reshape(batch * seq, d)
    for layer in range(norm_w.shape[0]):
        x2 = _layer(x2, batch, seq, *[p[layer] for p in params])
    return x2.reshape(batch, seq, d)
```

```python
import functools

import jax
import jax.numpy as jnp
from jax import lax
from jax.experimental import pallas as pl
from jax.experimental.pallas import tpu as pltpu

F32 = jnp.float32
BF16 = jnp.bfloat16

LANES = 128
HEAD_DIM = 64
RWKV_WIDTH = 1024
MOBA_WIDTH = 1024
LORA = 64
MOBA_BLOCK = 256
MOBA_TOPK = 3
RMS_EPS = 1e-6
GN_EPS = 64e-5
NEG_INF = -1e30
LOG2_E = 1.4426950408889634
CHUNK = 64
VMEM_LIMIT = 56 * 1024 * 1024
RWKV_PAIRS_PER_STEP = 2
PREP_UNROLL = 4
SCAN_UNROLL = 2
NORM_ROWS = 256
NORM_UNROLL = 2

SLAB_R, SLAB_K, SLAB_V, SLAB_ZA = 0, 8, 16, 24
SLAB_Q, SLAB_KQ, SLAB_VQ, SLAB_ZB = 32, 40, 48, 56
SLAB_G = 64
N_SLABS = 96


def _split2(x):
    hi = x.astype(BF16)
    lo = (x - hi.astype(F32)).astype(BF16)
    return hi, lo


def _split3(x):
    hi = x.astype(BF16)
    r1 = x - hi.astype(F32)
    mid = r1.astype(BF16)
    lo = (r1 - mid.astype(F32)).astype(BF16)
    return hi, mid, lo


def _dot(a, b):
    return jnp.dot(a, b, preferred_element_type=F32)


def _dot_nt(a, b):
    return lax.dot_general(a, b, (((1,), (1,)), ((), ())), preferred_element_type=F32)


def _dot_tn(a, b):
    return lax.dot_general(a, b, (((0,), (0,)), ((), ())), preferred_element_type=F32)


def _lane_iota(shape):
    return lax.broadcasted_iota(jnp.int32, shape, len(shape) - 1)


def _row_iota(shape):
    return lax.broadcasted_iota(jnp.int32, shape, 0)


def _head_ones():
    r = _row_iota((LANES, LANES))
    c = _lane_iota((LANES, LANES))
    return jnp.where((r < HEAD_DIM) == (c < HEAD_DIM), 1.0, 0.0).astype(BF16)


def _head_sum(x, ones_bd):
    hi, mid, lo = _split3(x)
    return _dot(hi, ones_bd) + _dot(mid, ones_bd) + _dot(lo, ones_bd)


def _bd(x):
    first = _lane_iota(x.shape) < HEAD_DIM
    zero = jnp.zeros_like(x)
    return jnp.concatenate([jnp.where(first, x, zero), jnp.where(first, zero, x)], axis=0)


def _fold(m):
    first = _lane_iota((HEAD_DIM, LANES)) < HEAD_DIM
    return jnp.where(first, m[:HEAD_DIM], m[HEAD_DIM:])


def _in_proj_kernel(x_ref, nw_ref, w_ref, wl_ref, p_ref, pl_ref, h_ref, *, slabs):
    @pl.when(pl.program_id(1) == 0)
    def _():
        x = x_ref[...]
        ms = jnp.mean(x * x, axis=-1, keepdims=True)
        h = ((x * lax.rsqrt(ms + RMS_EPS)) * nw_ref[...]).astype(BF16)
        h_ref[...] = h
        pl_ref[...] = _dot(h, wl_ref[...])

    acc = _dot(h_ref[...], w_ref[...])
    for c in range(slabs):
        p_ref[c] = acc[:, c * LANES:(c + 1) * LANES]


def _in_proj(x2, norm_w, w_main, w_lora, *, tm, tn):
    m, d = x2.shape
    n = w_main.shape[1]
    slabs = tn // LANES
    return pl.pallas_call(
        functools.partial(_in_proj_kernel, slabs=slabs),
        grid=(m // tm, n // tn),
        in_specs=[
            pl.BlockSpec((tm, d), lambda i, j: (i, 0)),
            pl.BlockSpec((1, d), lambda i, j: (0, 0)),
            pl.BlockSpec((d, tn), lambda i, j: (0, j)),
            pl.BlockSpec((d, LANES), lambda i, j: (0, 0)),
        ],
        out_specs=[
            pl.BlockSpec((slabs, tm, LANES), lambda i, j: (j, i, 0)),
            pl.BlockSpec((tm, LANES), lambda i, j: (i, 0)),
        ],
        out_shape=[
            jax.ShapeDtypeStruct((n // LANES, m, LANES), F32),
            jax.ShapeDtypeStruct((m, LANES), F32),
        ],
        scratch_shapes=[pltpu.VMEM((tm, d), BF16)],
        compiler_params=pltpu.CompilerParams(
            dimension_semantics=("arbitrary", "arbitrary"), vmem_limit_bytes=VMEM_LIMIT),
        name="in_proj",
    )(x2, norm_w, w_main, w_lora)


def _mm(a, b):
    return _dot(a.astype(BF16), _bd(b).astype(BF16))


def _mm_nt(a, b):
    return _dot_nt(a.astype(BF16), _bd(b).astype(BF16))


def _mm_tn(a, b):
    return _fold(_dot_tn(a.astype(BF16), b.astype(BF16)))


def _rwkv_kernel(r_ref, k_ref, v_ref, z_ref, lo_ref, prm_ref, wup_ref, o_ref,
                 whi_ref, wlo_ref, hadd_ref, rp_ref, y0_ref, bonus_ref, *, seq, pairs):
    n_chunks = seq // CHUNK
    L = CHUNK
    ones_bd = _head_ones()
    row = _row_iota((L, LANES))
    lane = _lane_iota((L, LANES))
    col = lane & (HEAD_DIM - 1)
    strict = col < row
    incl = col <= row
    eye = col == row
    first_lanes = lane < HEAD_DIM
    tri = jnp.where(_lane_iota((L, L)) <= _row_iota((L, L)), 1.0, 0.0).astype(BF16)

    def shifted(ref, c, mu):
        start = pl.multiple_of(c * L, L)
        cur = ref[pl.ds(start, L), :]
        prev_start = pl.multiple_of(jnp.maximum(c * L - 8, 0), 8)
        tail = ref[pl.ds(prev_start, 8), :][7:8, :]
        tail = tail * jnp.where(c > 0, 1.0, 0.0)
        prev = jnp.where(row == 0, tail, pltpu.roll(cur, 1, 0))
        return cur + (prev - cur) * mu

    def prepare(chains):
        def each(fn, *lists):
            return [fn(*args) for args in zip(*lists)]

        prms = [prm_ref[pr] for _, pr in chains]
        rows = [pl.ds(pl.multiple_of(c * L, L), L) for c, _ in chains]
        r = [shifted(r_ref.at[pr], c, prm[0:1]) for (c, pr), prm in zip(chains, prms)]
        k = [shifted(k_ref.at[pr], c, prm[1:2]) for (c, pr), prm in zip(chains, prms)]
        v = [shifted(v_ref.at[pr], c, prm[2:3]) for (c, pr), prm in zip(chains, prms)]
        xl = [shifted(lo_ref, c, prm[10:11]) for (c, pr), prm in zip(chains, prms)]

        feat = each(lambda x: _split2(jnp.where(first_lanes, jnp.tanh(x), x)), xl)
        wups = [_split2(wup_ref[pr]) for _, pr in chains]
        up = each(lambda f, w: _dot(f[0], w[0]) + _dot(f[1], w[0]) + _dot(f[0], w[1]), feat, wups)

        def decay_log(u, prm):
            neg = -(prm[3:4] + u[:, :LANES])
            softplus = jnp.maximum(neg, 0.0) + jnp.log(1.0 + jnp.exp(-jnp.abs(neg)))
            return -jnp.exp(-softplus - 0.5)

        logw = each(decay_log, up, prms)
        a = each(lambda u, prm: 1.0 / (1.0 + jnp.exp(-(prm[4:5] + u[:, LANES:]))), up, prms)

        kk = each(lambda k_, prm: k_ * prm[5:6], k, prms)
        ksq = each(lambda x: _head_sum(x * x, ones_bd), kk)
        kk = each(lambda x, s: x / jnp.maximum(jnp.sqrt(s), 1e-12), kk, ksq)
        k2 = each(lambda k_, a_, prm: k_ * (1.0 + (a_ - 1.0) * prm[6:7]), k, a, prms)
        b = each(lambda x, a_: x * a_, kk, a)
        rk = each(lambda r_, k_, prm: _head_sum(r_ * k_ * prm[7:8], ones_bd), r, k2, prms)
        for (c, pr), rw, x, v_ in zip(chains, rows, rk, v):
            bonus_ref[pr, rw, :] = x * v_

        lsplit = each(_split3, logw)
        cum = each(lambda s: _dot(tri, s[0]) + _dot(tri, s[1]) + _dot(tri, s[2]), lsplit)
        cum_l = each(lambda x: x[L - 1:L, :], cum)
        g_inv = each(lambda x: jnp.exp(-x), cum)
        g_rest = each(lambda x, xl_: jnp.exp(xl_ - x), cum, cum_l)
        at = each(lambda kk_, x, lw: -kk_ * jnp.exp(x - lw), kk, cum, logw)
        rt = each(lambda r_, x: r_ * jnp.exp(x), r, cum)
        bt = each(lambda b_, g: b_ * g, b, g_inv)
        kt = each(lambda k_, g: k_ * g, k2, g_inv)
        bh = each(lambda b_, g: b_ * g, b, g_rest)
        kh = each(lambda k_, g: k_ * g, k2, g_rest)

        lhs = each(lambda x, y: jnp.concatenate([x, y], axis=0), at, rt)
        ab = each(_mm_nt, lhs, bt)
        ak = each(_mm_nt, lhs, kt)
        zero = jnp.zeros((L, LANES), F32)
        n_ab = each(lambda x: jnp.where(strict, x[:L], zero), ab)
        n_ak = each(lambda x: jnp.where(strict, x[:L], zero), ak)
        a_rb = each(lambda x: jnp.where(incl, x[L:], zero), ab)
        a_rk = each(lambda x: jnp.where(incl, x[L:], zero), ak)

        t = each(lambda x: jnp.where(eye, 1.0, 0.0) + x, n_ab)
        pw = each(_mm, n_ab, n_ab)
        steps = CHUNK.bit_length() - 2
        for s in range(steps):
            t = each(lambda t_, p_: t_ + _mm(t_, p_), t, pw)
            if s + 1 < steps:
                pw = each(_mm, pw, pw)

        akv = each(_mm, n_ak, v)
        p = each(_mm, t, at)
        q = each(_mm, t, akv)
        rp = each(lambda x, m, p_: x + _mm(m, p_), rt, a_rb, p)
        y0 = each(lambda m1, q_, m2, v_: _mm(m1, q_) + _mm(m2, v_), a_rb, q, a_rk, v)
        gt = each(lambda xl_, p_, b_: jnp.where(eye, jnp.exp(xl_), 0.0) + _mm_tn(p_, b_), cum_l, p, bh)
        hadd = each(lambda q_, b_, v_, k_: _mm_tn(q_, b_) + _mm_tn(v_, k_), q, bh, v, kh)
        for i, (c, pr) in enumerate(chains):
            rp_ref[pr, rows[i], :] = rp[i]
            y0_ref[pr, rows[i], :] = y0[i]
            hadd_ref[pr, rows[i], :] = hadd[i]
            w_hi, w_lo = _split2(_bd(gt[i]))
            wrows = pl.ds(pl.multiple_of(c * LANES, LANES), LANES)
            whi_ref[pr, wrows, :] = w_hi
            wlo_ref[pr, wrows, :] = w_lo

    def advance(c, states):
        rows = pl.ds(pl.multiple_of(c * L, L), L)
        wrows = pl.ds(pl.multiple_of(c * LANES, LANES), LANES)
        split = [_split2(s) for s in states]
        new = []
        for pr in range(pairs):
            s_hi, s_lo = split[pr]
            w_hi = whi_ref[pr, wrows, :]
            new.append(_dot(s_hi, w_hi) + _dot(s_lo, w_hi) + _dot(s_hi, wlo_ref[pr, wrows, :])
                       + hadd_ref[pr, rows, :])
        for pr in range(pairs):
            y0_ref[pr, rows, :] = _mm_nt(rp_ref[pr, rows, :], states[pr]) + y0_ref[pr, rows, :]
        return tuple(new)

    def finish(tiles):
        rows = [pl.ds(pl.multiple_of(t * NORM_ROWS, NORM_ROWS), NORM_ROWS) for t, _ in tiles]
        y = [y0_ref[pr, rw, :] for (_, pr), rw in zip(tiles, rows)]
        mean = [_head_sum(x, ones_bd) * (1.0 / HEAD_DIM) for x in y]
        yc = [x - m for x, m in zip(y, mean)]
        var = [_head_sum(x * x, ones_bd) * (1.0 / HEAD_DIM) for x in yc]
        for (_, pr), rw, x, s in zip(tiles, rows, yc, var):
            prm = prm_ref[pr]
            yn = x * lax.rsqrt(s + GN_EPS) * prm[8:9] + prm[9:10] + bonus_ref[pr, rw, :]
            z = z_ref[pr, rw, :]
            o_ref[rw, pr * LANES:(pr + 1) * LANES] = (yn * (z / (1.0 + jnp.exp(-z)))).astype(o_ref.dtype)

    def phase1(i, carry):
        prepare([(i * PREP_UNROLL + u, pr) for u in range(PREP_UNROLL) for pr in range(pairs)])
        return carry

    lax.fori_loop(0, n_chunks // PREP_UNROLL, phase1, 0)

    def phase2(i, states):
        for u in range(SCAN_UNROLL):
            states = advance(i * SCAN_UNROLL + u, states)
        return states

    lax.fori_loop(0, n_chunks // SCAN_UNROLL, phase2,
                  tuple(jnp.zeros((HEAD_DIM, LANES), F32) for _ in range(pairs)))

    def phase3(i, carry):
        finish([(i * NORM_UNROLL + u, pr) for u in range(NORM_UNROLL) for pr in range(pairs)])
        return carry

    lax.fori_loop(0, seq // NORM_ROWS // NORM_UNROLL, phase3, 0)


def _rwkv(p3, plora, prm, wup, *, batch, seq):
    pairs = RWKV_PAIRS_PER_STEP
    groups = RWKV_WIDTH // LANES // pairs
    n_chunks = seq // CHUNK
    assert n_chunks % PREP_UNROLL == 0 and n_chunks % SCAN_UNROLL == 0
    assert seq % (NORM_ROWS * NORM_UNROLL) == 0

    def slab(base):
        return pl.BlockSpec((pairs, seq, LANES), lambda b, h: (base // pairs + h, b, 0))

    def per_pair(rows, dtype):
        return pltpu.VMEM((pairs, rows, LANES), dtype)

    return pl.pallas_call(
        functools.partial(_rwkv_kernel, seq=seq, pairs=pairs),
        grid=(batch, groups),
        in_specs=[
            slab(SLAB_R), slab(SLAB_K), slab(SLAB_V), slab(SLAB_ZA),
            pl.BlockSpec((seq, LANES), lambda b, h: (b, 0)),
            pl.BlockSpec((pairs, 16, LANES), lambda b, h: (h, 0, 0)),
            pl.BlockSpec((pairs, LANES, 2 * LANES), lambda b, h: (h, 0, 0)),
        ],
        out_specs=pl.BlockSpec((seq, pairs * LANES), lambda b, h: (b, h)),
        out_shape=jax.ShapeDtypeStruct((batch * seq, RWKV_WIDTH), BF16),
        scratch_shapes=[
            per_pair(n_chunks * LANES, BF16), per_pair(n_chunks * LANES, BF16),
            per_pair(seq, F32), per_pair(seq, F32), per_pair(seq, F32), per_pair(seq, F32),
        ],
        compiler_params=pltpu.CompilerParams(
            dimension_semantics=("arbitrary", "arbitrary"), vmem_limit_bytes=VMEM_LIMIT),
        name="rwkv7_mix",
    )(p3, p3, p3, p3, plora, prm, wup)


def _moba_kernel(q_ref, k_ref, v_ref, z_ref, qw_ref, kw_ref, o_ref,
                 q0_ref, q1_ref, ks_ref, vt_ref, gate_ref, *, seq):
    blk = MOBA_BLOCK
    nb = seq // blk
    ones_bd = _head_ones()
    first = _lane_iota((seq, LANES)) < HEAD_DIM

    def normed(ref, w_ref):
        x = ref[...]
        hi, lo = _split2(x * x)
        ms = (_dot(hi, ones_bd) + _dot(lo, ones_bd)) * (1.0 / HEAD_DIM)
        return x * lax.rsqrt(ms + RMS_EPS) * w_ref[...]

    qn = normed(q_ref, qw_ref)
    kn = normed(k_ref, kw_ref)
    qs = qn * (HEAD_DIM ** -0.5 * LOG2_E)
    q0_ref[...] = jnp.where(first, qs, 0.0).astype(BF16)
    q1_ref[...] = jnp.where(first, 0.0, qs).astype(BF16)
    ks_ref[...] = kn.astype(BF16)
    eye = jnp.where(_lane_iota((LANES, LANES)) == _row_iota((LANES, LANES)), 1.0, 0.0).astype(BF16)
    vt_ref[...] = _dot_nt(eye, v_ref[...].astype(BF16)).astype(BF16)

    km = jnp.mean(kn.reshape(nb, blk, LANES), axis=1)
    first_nb = _lane_iota((nb, LANES)) < HEAD_DIM
    km_bd = jnp.concatenate([jnp.where(first_nb, km, 0.0), jnp.where(first_nb, 0.0, km)], axis=0)
    km_hi, km_lo = _split2(km_bd)
    q_hi, q_lo = _split2(qn)
    gate_ref[...] = _dot_nt(km_hi, q_hi) + _dot_nt(km_hi, q_lo) + _dot_nt(km_lo, q_hi)

    causal_t = _row_iota((blk, blk)) <= _lane_iota((blk, blk))
    blk_row = _row_iota((nb, blk))
    top_rows = _row_iota((LANES, blk)) < HEAD_DIM

    def scores(i, h):
        qh_ref = q0_ref if h == 0 else q1_ref
        return _dot_nt(ks_ref[0:(i + 1) * blk, :], qh_ref[i * blk:(i + 1) * blk, :])

    def attend(i, h, s):
        pieces = []
        if i > MOBA_TOPK:
            gate = gate_ref[h * nb:(h + 1) * nb, i * blk:(i + 1) * blk]
        for j in range(i):
            sj = s[j * blk:(j + 1) * blk, :]
            if i > MOBA_TOPK:
                gj = gate[j:j + 1, :]
                beats = (blk_row < i) & ((gate > gj) | ((gate == gj) & (blk_row < j)))
                rank = jnp.sum(jnp.where(beats, 1.0, 0.0), axis=0, keepdims=True)
                sj = jnp.where(rank < MOBA_TOPK, sj, NEG_INF)
            pieces.append(sj)
        pieces.append(jnp.where(causal_t, s[i * blk:, :], NEG_INF))
        s = jnp.concatenate(pieces, axis=0) if i > 0 else pieces[0]
        m = jnp.max(s, axis=0, keepdims=True)
        e = jnp.exp2(s - m)
        denom = jnp.sum(e, axis=0, keepdims=True)
        return _dot(vt_ref[:, 0:(i + 1) * blk], e.astype(BF16)) * (1.0 / denom)

    units = [(i, h) for i in range(nb) for h in range(2)]
    s_next = scores(*units[0])
    outs = []
    for n, (i, h) in enumerate(units):
        s_cur = s_next
        if n + 1 < len(units):
            s_next = scores(*units[n + 1])
        outs.append(attend(i, h, s_cur))
        if h == 1:
            rows = slice(i * blk, (i + 1) * blk)
            o = jnp.where(top_rows, outs[-2], outs[-1]).T
            z = z_ref[rows, :]
            o_ref[rows, :] = (o * (z / (1.0 + jnp.exp(-z)))).astype(o_ref.dtype)


def _moba(p3, qw, kw, *, batch, seq):
    hp = MOBA_WIDTH // LANES

    def slab(base):
        return pl.BlockSpec((None, seq, LANES), lambda b, h: (base + h, b, 0))

    return pl.pallas_call(
        functools.partial(_moba_kernel, seq=seq),
        grid=(batch, hp),
        in_specs=[
            slab(SLAB_Q), slab(SLAB_KQ), slab(SLAB_VQ), slab(SLAB_ZB),
            pl.BlockSpec((1, LANES), lambda b, h: (0, 0)),
            pl.BlockSpec((1, LANES), lambda b, h: (0, 0)),
        ],
        out_specs=pl.BlockSpec((seq, LANES), lambda b, h: (b, h)),
        out_shape=jax.ShapeDtypeStruct((batch * seq, MOBA_WIDTH), BF16),
        scratch_shapes=[
            pltpu.VMEM((seq, LANES), BF16), pltpu.VMEM((seq, LANES), BF16), pltpu.VMEM((seq, LANES), BF16),
            pltpu.VMEM((LANES, seq), BF16), pltpu.VMEM((2 * (seq // MOBA_BLOCK), seq), F32),
        ],
        compiler_params=pltpu.CompilerParams(
            dimension_semantics=("arbitrary", "arbitrary"), vmem_limit_bytes=VMEM_LIMIT),
        name="moba_attention",
    )(p3, p3, p3, p3, qw, kw)


def _out_kernel(x_ref, ya_ref, yb_ref, g_ref, wa_ref, wb_ref, wo_ref, o_ref, *, d_model):
    n = d_model // LANES
    pa = _dot(ya_ref[...], wa_ref[...])
    pb = _dot(yb_ref[...], wb_ref[...])
    ga = jnp.concatenate([g_ref[c] for c in range(n)], axis=1)
    gb = jnp.concatenate([g_ref[n + c] for c in range(n)], axis=1)
    merged = pa / (1.0 + jnp.exp(-ga)) + pb / (1.0 + jnp.exp(-gb))
    o_ref[...] = x_ref[...] + _dot(merged.astype(BF16), wo_ref[...])


def _out_proj(x2, ya, yb, p3, wa, wb, wo, *, tm):
    m, d = x2.shape
    n_g = 2 * d // LANES
    const = dict(pipeline_mode=pl.Buffered(1))
    return pl.pallas_call(
        functools.partial(_out_kernel, d_model=d),
        grid=(m // tm,),
        in_specs=[
            pl.BlockSpec((tm, d), lambda i: (i, 0)),
            pl.BlockSpec((tm, RWKV_WIDTH), lambda i: (i, 0)),
            pl.BlockSpec((tm, MOBA_WIDTH), lambda i: (i, 0)),
            pl.BlockSpec((n_g, tm, LANES), lambda i: (SLAB_G // n_g, i, 0)),
            pl.BlockSpec((RWKV_WIDTH, d), lambda i: (0, 0), **const),
            pl.BlockSpec((MOBA_WIDTH, d), lambda i: (0, 0), **const),
            pl.BlockSpec((d, d), lambda i: (0, 0), **const),
        ],
        out_specs=pl.BlockSpec((tm, d), lambda i: (i, 0)),
        out_shape=jax.ShapeDtypeStruct((m, d), F32),
        compiler_params=pltpu.CompilerParams(
            dimension_semantics=("arbitrary",), vmem_limit_bytes=VMEM_LIMIT),
        name="merge_out_proj",
    )(x2, ya, yb, p3, wa, wb, wo)


def _layer(x2, batch, seq, norm_w, w_in, mu_r, mu_k, mu_v, mu_w, mu_a, w0, w_decay_up, a0, w_iclr_up,
           k_k, k_a, r_k, gn_w, gn_b, q_norm_w, k_norm_w, w_proj_rwkv, w_proj_moba, w_out):
    d = x2.shape[1]
    lora_at = 4 * RWKV_WIDTH
    w_main = jnp.concatenate([w_in[:, :lora_at], w_in[:, lora_at + 2 * LORA:]], axis=1).astype(BF16)
    w_lora = w_in[:, lora_at:lora_at + 2 * LORA].astype(BF16)
    assert w_main.shape[1] == N_SLABS * LANES and d == (N_SLABS * LANES - SLAB_G * LANES) // 2

    p3, plora = _in_proj(x2, norm_w.reshape(1, d), w_main, w_lora, tm=min(1024, x2.shape[0]), tn=512)

    hp = RWKV_WIDTH // LANES
    vecs = jnp.stack([mu_r, mu_k, mu_v, w0, a0, k_k, k_a, r_k, gn_w, gn_b]).reshape(10, hp, LANES)
    mu_l = jnp.broadcast_to(jnp.concatenate([mu_w, mu_a]).reshape(1, 1, LANES), (1, hp, LANES))
    prm = jnp.concatenate([vecs, mu_l, jnp.zeros((5, hp, LANES), F32)], axis=0).transpose(1, 0, 2)
    zeros = jnp.zeros((hp, LORA, LANES), F32)
    wd = w_decay_up.reshape(LORA, hp, LANES).transpose(1, 0, 2)
    wa_up = w_iclr_up.reshape(LORA, hp, LANES).transpose(1, 0, 2)
    wup = jnp.concatenate([jnp.concatenate([wd, zeros], axis=2),
                           jnp.concatenate([zeros, wa_up], axis=2)], axis=1)

    ya = _rwkv(p3, plora, prm, wup, batch=batch, seq=seq)
    qw = jnp.tile(q_norm_w, 2).reshape(1, LANES)
    kw = jnp.tile(k_norm_w, 2).reshape(1, LANES)
    yb = _moba(p3, qw, kw, batch=batch, seq=seq)
    return _out_proj(x2, ya, yb, p3, w_proj_rwkv.astype(BF16), w_proj_moba.astype(BF16),
                     w_out.astype(BF16), tm=256)


def kernel(x, norm_w, w_in, mu_r, mu_k, mu_v, mu_w, mu_a, w0, w_decay_up, a0, w_iclr_up, k_k, k_a, r_k,
           gn_w, gn_b, q_norm_w, k_norm_w, w_proj_rwkv, w_proj_moba, w_out):
    batch, seq, d = x.shape
    assert seq % MOBA_BLOCK == 0 and seq % CHUNK == 0
    params = (norm_w, w_in, mu_r, mu_k, mu_v, mu_w, mu_a, w0, w_decay_up, a0, w_iclr_up, k_k, k_a, r_k,
              gn_w, gn_b, q_norm_w, k_norm_w, w_proj_rwkv, w_proj_moba, w_out)
    x2 = x.reshape(batch * seq, d)
    for layer in range(norm_w.shape[0]):
        x2 = _layer(x2, batch, seq, *[p[layer] for p in params])
    return x2.reshape(batch, seq, d)
```

```python
import functools

import jax
import jax.numpy as jnp
from jax import lax
from jax.experimental import pallas as pl
from jax.experimental.pallas import tpu as pltpu

F32 = jnp.float32
BF16 = jnp.bfloat16

LANES = 128
HEAD_DIM = 64
RWKV_WIDTH = 1024
MOBA_WIDTH = 1024
LORA = 64
MOBA_BLOCK = 256
MOBA_TOPK = 3
RMS_EPS = 1e-6
GN_EPS = 64e-5
NEG_INF = -1e30
LOG2_E = 1.4426950408889634
CHUNK = 64
VMEM_LIMIT = 56 * 1024 * 1024
RWKV_PAIRS_PER_STEP = 2
LAG = 8
NORM_ROWS = 256
NORM_UNROLL = 2
SHIFT_PAD = 8

SLAB_R, SLAB_K, SLAB_V, SLAB_ZA = 0, 8, 16, 24
SLAB_Q, SLAB_KQ, SLAB_VQ, SLAB_ZB = 32, 40, 48, 56
SLAB_G = 64
N_SLABS = 96


def _split2(x):
    hi = x.astype(BF16)
    lo = (x - hi.astype(F32)).astype(BF16)
    return hi, lo


def _split_lanes(x):
    hi, lo = _split2(x)
    return jnp.concatenate([hi, lo], axis=1)


def _dot(a, b):
    return jnp.dot(a, b, preferred_element_type=F32)


def _dot_nt(a, b):
    return lax.dot_general(a, b, (((1,), (1,)), ((), ())), preferred_element_type=F32)


def _dot_tn(a, b):
    return lax.dot_general(a, b, (((0,), (0,)), ((), ())), preferred_element_type=F32)


def _lane_iota(shape):
    return lax.broadcasted_iota(jnp.int32, shape, len(shape) - 1)


def _row_iota(shape):
    return lax.broadcasted_iota(jnp.int32, shape, 0)


def _head_ones():
    r = _row_iota((2 * LANES, LANES)) & (LANES - 1)
    c = _lane_iota((2 * LANES, LANES))
    return jnp.where((r < HEAD_DIM) == (c < HEAD_DIM), 1.0, 0.0).astype(BF16)


def _head_sum(x, ones2):
    return _dot(_split_lanes(x), ones2)


def _bd(x):
    first = _lane_iota(x.shape) < HEAD_DIM
    zero = jnp.zeros_like(x)
    return jnp.concatenate([jnp.where(first, x, zero), jnp.where(first, zero, x)], axis=0)


def _fold(m):
    first = _lane_iota((HEAD_DIM, LANES)) < HEAD_DIM
    return jnp.where(first, m[:HEAD_DIM], m[HEAD_DIM:])


def _in_proj_kernel(x_ref, nw_ref, w_ref, wl_ref, p_ref, pl_ref, h_ref, *, slabs):
    @pl.when(pl.program_id(1) == 0)
    def _():
        x = x_ref[...]
        ms = jnp.mean(x * x, axis=-1, keepdims=True)
        h = ((x * lax.rsqrt(ms + RMS_EPS)) * nw_ref[...]).astype(BF16)
        h_ref[...] = h
        pl_ref[...] = _dot(h, wl_ref[...])

    acc = _dot(h_ref[...], w_ref[...])
    for c in range(slabs):
        p_ref[c] = acc[:, c * LANES:(c + 1) * LANES]


def _in_proj(x2, norm_w, w_main, w_lora, *, tm, tn):
    m, d = x2.shape
    n = w_main.shape[1]
    slabs = tn // LANES
    return pl.pallas_call(
        functools.partial(_in_proj_kernel, slabs=slabs),
        grid=(m // tm, n // tn),
        in_specs=[
            pl.BlockSpec((tm, d), lambda i, j: (i, 0)),
            pl.BlockSpec((1, d), lambda i, j: (0, 0)),
            pl.BlockSpec((d, tn), lambda i, j: (0, j)),
            pl.BlockSpec((d, LANES), lambda i, j: (0, 0)),
        ],
        out_specs=[
            pl.BlockSpec((slabs, tm, LANES), lambda i, j: (j, i, 0)),
            pl.BlockSpec((tm, LANES), lambda i, j: (i, 0)),
        ],
        out_shape=[
            jax.ShapeDtypeStruct((n // LANES, m, LANES), F32),
            jax.ShapeDtypeStruct((m, LANES), F32),
        ],
        scratch_shapes=[pltpu.VMEM((tm, d), BF16)],
        compiler_params=pltpu.CompilerParams(
            dimension_semantics=("arbitrary", "arbitrary"), vmem_limit_bytes=VMEM_LIMIT),
        name="in_proj",
    )(x2, norm_w, w_main, w_lora)


def _mm(a, b):
    return _dot(a.astype(BF16), _bd(b).astype(BF16))


def _mm_nt(a, b):
    return _dot_nt(a.astype(BF16), _bd(b).astype(BF16))


def _mm_tn(a, b):
    return _fold(_dot_tn(a.astype(BF16), b.astype(BF16)))


def _rwkv_kernel(r_ref, k_ref, v_ref, z_ref, lo_ref, prm_ref, wup_ref, o_ref,
                 whi_ref, wlo_ref, hadd_ref, rp_ref, y0_ref, bonus_ref, pad_ref, *, seq, pairs):
    n_chunks = seq // CHUNK
    L = CHUNK
    ones_bd = _head_ones()
    row = _row_iota((L, LANES))
    lane = _lane_iota((L, LANES))
    col = lane & (HEAD_DIM - 1)
    strict = col < row
    incl = col <= row
    eye = col == row
    first_lanes = lane < HEAD_DIM
    tri = jnp.where(_lane_iota((L, L)) <= _row_iota((L, L)), 1.0, 0.0).astype(BF16)
    tri2 = jnp.concatenate([tri, tri], axis=1)
    row2 = _row_iota((LANES, LANES))
    lane2 = _lane_iota((LANES, LANES))
    eye2 = row2 == lane2
    same_head = (row2 < HEAD_DIM) == (lane2 < HEAD_DIM)

    shift_srcs = [ref.at[pr] for pr in range(pairs) for ref in (r_ref, k_ref, v_ref)] + [lo_ref]
    for n, src in enumerate(shift_srcs):
        pad_ref[n, 0:SHIFT_PAD, :] = jnp.zeros((SHIFT_PAD, LANES), F32)

    def copy_rows(t, carry):
        rows = pl.ds(pl.multiple_of(t * NORM_ROWS, NORM_ROWS), NORM_ROWS)
        dst = pl.ds(pl.multiple_of(t * NORM_ROWS + SHIFT_PAD, SHIFT_PAD), NORM_ROWS)
        for n, src in enumerate(shift_srcs):
            pad_ref[n, dst, :] = src[rows, :]
        return carry

    lax.fori_loop(0, seq // NORM_ROWS, copy_rows, 0)

    def shifted(n, c, mu):
        cur = shift_srcs[n][pl.ds(pl.multiple_of(c * L, L), L), :]
        prev = pad_ref[n, pl.ds(c * L + (SHIFT_PAD - 1), L), :]
        return cur + (prev - cur) * mu

    def aligned(x, m):
        return x if isinstance(x, int) else pl.multiple_of(x, m)

    def prepare(chains, hooks):
        hooks = list(hooks)

        def tick():
            if hooks:
                hooks.pop(0)()

        def each(fn, *lists):
            return [fn(*args) for args in zip(*lists)]

        prms = [prm_ref[pr] for _, pr in chains]
        rows = [pl.ds(pl.multiple_of(c * L, L), L) for c, _ in chains]
        slot_rows = [pl.ds(pl.multiple_of((c + LAG) * L, L), L) for c, _ in chains]
        r = [shifted(3 * pr, c, prm[0:1]) for (c, pr), prm in zip(chains, prms)]
        k = [shifted(3 * pr + 1, c, prm[1:2]) for (c, pr), prm in zip(chains, prms)]
        v = [shifted(3 * pr + 2, c, prm[2:3]) for (c, pr), prm in zip(chains, prms)]
        xl = [shifted(3 * pairs, c, prm[10:11]) for (c, pr), prm in zip(chains, prms)]

        feat = each(lambda x: _split_lanes(jnp.where(first_lanes, jnp.tanh(x), x)), xl)
        wups = [_split2(wup_ref[pr]) for _, pr in chains]
        up = each(lambda f, w: _dot(f, jnp.concatenate([w[0], w[0]], axis=0)) + _dot(f[:, :LANES], w[1]),
                  feat, wups)
        tick()

        def decay_log(u, prm):
            neg = -(prm[3:4] + u[:, :LANES])
            softplus = jnp.maximum(neg, 0.0) + jnp.log(1.0 + jnp.exp(-jnp.abs(neg)))
            return -jnp.exp(-softplus - 0.5)

        logw = each(decay_log, up, prms)
        a = each(lambda u, prm: 1.0 / (1.0 + jnp.exp(-(prm[4:5] + u[:, LANES:]))), up, prms)

        kk = each(lambda k_, prm: k_ * prm[5:6], k, prms)
        ksq = each(lambda x: _head_sum(x * x, ones_bd), kk)
        kk = each(lambda x, s: x / jnp.maximum(jnp.sqrt(s), 1e-12), kk, ksq)
        k2 = each(lambda k_, a_, prm: k_ * (1.0 + (a_ - 1.0) * prm[6:7]), k, a, prms)
        b = each(lambda x, a_: x * a_, kk, a)
        rk = each(lambda r_, k_, prm: _head_sum(r_ * k_ * prm[7:8], ones_bd), r, k2, prms)
        for (c, pr), rw, x, v_ in zip(chains, rows, rk, v):
            bonus_ref[pr, rw, :] = x * v_
        tick()

        cum = each(lambda x: _dot(tri2, jnp.concatenate(_split2(x), axis=0)), logw)
        tick()
        cum_l = each(lambda x: x[L - 1:L, :], cum)
        g_inv = each(lambda x: jnp.exp(-x), cum)
        g_rest = each(lambda x, xl_: jnp.exp(xl_ - x), cum, cum_l)
        at = each(lambda kk_, x, lw: -kk_ * jnp.exp(x - lw), kk, cum, logw)
        rt = each(lambda r_, x: r_ * jnp.exp(x), r, cum)
        bt = each(lambda b_, g: b_ * g, b, g_inv)
        kt = each(lambda k_, g: k_ * g, k2, g_inv)
        bh = each(lambda b_, g: b_ * g, b, g_rest)
        kh = each(lambda k_, g: k_ * g, k2, g_rest)

        lhs = each(lambda x, y: jnp.concatenate([x, y], axis=0), at, rt)
        ab = each(_mm_nt, lhs, bt)
        ak = each(_mm_nt, lhs, kt)
        tick()
        zero = jnp.zeros((L, LANES), F32)
        n_ab = each(lambda x: jnp.where(strict, x[:L], zero), ab)
        n_ak = each(lambda x: jnp.where(strict, x[:L], zero), ak)
        a_rb = each(lambda x: jnp.where(incl, x[L:], zero), ab)
        a_rk = each(lambda x: jnp.where(incl, x[L:], zero), ak)

        t = each(lambda x: jnp.where(eye, 1.0, 0.0) + x, n_ab)
        pw = each(_mm, n_ab, n_ab)
        steps = CHUNK.bit_length() - 2
        for s in range(steps):
            t = each(lambda t_, p_: t_ + _mm(t_, p_), t, pw)
            if s + 1 < steps:
                pw = each(_mm, pw, pw)
            if s % 2 == 0:
                tick()

        akv = each(_mm, n_ak, v)
        p = each(_mm, t, at)
        q = each(_mm, t, akv)
        tick()
        rp = each(lambda x, m, p_: x + _mm(m, p_), rt, a_rb, p)
        y0 = each(lambda m1, q_, m2, v_: _dot(jnp.concatenate([m1, m2], axis=1).astype(BF16),
                                              jnp.concatenate([_bd(q_), _bd(v_)], axis=0).astype(BF16)),
                  a_rb, q, a_rk, v)
        hadd = each(lambda q_, b_, v_, k_: _fold(_dot_tn(jnp.concatenate([q_, v_], axis=0).astype(BF16),
                                                         jnp.concatenate([b_, k_], axis=0).astype(BF16))),
                    q, bh, v, kh)
        w = each(lambda xl_, p_, b_: jnp.where(eye2, jnp.exp(xl_), 0.0)
                 + jnp.where(same_head, _dot_tn(p_.astype(BF16), b_.astype(BF16)), 0.0), cum_l, p, bh)
        while hooks:
            tick()
        for i, (c, pr) in enumerate(chains):
            rp_ref[pr, slot_rows[i], :] = rp[i]
            y0_ref[pr, slot_rows[i], :] = y0[i]
            hadd_ref[pr, slot_rows[i], :] = hadd[i]
            w_hi, w_lo = _split2(w[i])
            wrows = pl.ds(pl.multiple_of((c + LAG) * LANES, LANES), LANES)
            whi_ref[pr, wrows, :] = w_hi
            wlo_ref[pr, wrows, :] = w_lo

    def advance(slot, states):
        rows = pl.ds(aligned(slot * L, L), L)
        wrows = pl.ds(aligned(slot * LANES, LANES), LANES)
        split = [_split2(s) for s in states]
        new = []
        for pr in range(pairs):
            s_hi, s_lo = split[pr]
            w_hi = whi_ref[pr, wrows, :]
            new.append(_dot(s_hi, w_hi) + _dot(s_lo, w_hi) + _dot(s_hi, wlo_ref[pr, wrows, :])
                       + hadd_ref[pr, rows, :])
        for pr in range(pairs):
            y0_ref[pr, rows, :] = _mm_nt(rp_ref[pr, rows, :], states[pr]) + y0_ref[pr, rows, :]
        return tuple(new)

    def finish(tiles, hooks):
        hooks = list(hooks)

        def tick():
            if hooks:
                hooks.pop(0)()

        rows = [pl.ds(t * NORM_ROWS, NORM_ROWS) for t, _ in tiles]
        y = [y0_ref[pr, pl.ds(t * NORM_ROWS + LAG * L, NORM_ROWS), :] for t, pr in tiles]
        mean = [_head_sum(x, ones_bd) * (1.0 / HEAD_DIM) for x in y]
        tick()
        yc = [x - m for x, m in zip(y, mean)]
        var = [_head_sum(x * x, ones_bd) * (1.0 / HEAD_DIM) for x in yc]
        tick()
        for (_, pr), rw, x, s in zip(tiles, rows, yc, var):
            prm = prm_ref[pr]
            yn = x * lax.rsqrt(s + GN_EPS) * prm[8:9] + prm[9:10] + bonus_ref[pr, rw, :]
            z = z_ref[pr, rw, :]
            o_ref[rw, pr * LANES:(pr + 1) * LANES] = (yn * (z / (1.0 + jnp.exp(-z)))).astype(o_ref.dtype)
        while hooks:
            tick()

    for pr in range(pairs):
        whi_ref[pr, 0:LAG * LANES, :] = jnp.zeros((LAG * LANES, LANES), BF16)
        wlo_ref[pr, 0:LAG * LANES, :] = jnp.zeros((LAG * LANES, LANES), BF16)
        for ref in (hadd_ref, rp_ref, y0_ref):
            ref[pr, 0:LAG * L, :] = jnp.zeros((LAG * L, LANES), F32)

    def phase1(i, states):
        states = list(states)

        def step(u):
            def hook():
                states[:] = advance(i * LAG + u, tuple(states))
            return hook

        prepare([(i * LAG + u, pr) for u in range(LAG) for pr in range(pairs)],
                [step(u) for u in range(LAG)])
        return tuple(states)

    states = lax.fori_loop(0, n_chunks // LAG, phase1,
                           tuple(jnp.zeros((HEAD_DIM, LANES), F32) for _ in range(pairs)))
    states = list(states)

    def tail_step(u):
        def hook():
            states[:] = advance(n_chunks + u, tuple(states))
        return hook

    tail = [tail_step(u) for u in range(LAG)]
    n_tiles = seq // NORM_ROWS
    groups = [list(range(g, g + NORM_UNROLL)) for g in range(0, n_tiles, NORM_UNROLL)]
    n_early = sum(1 for g in groups if (g[-1] + 1) * NORM_ROWS <= seq - LAG * L)
    per_group = -(-LAG // max(n_early, 1))
    for g in groups:
        if (g[-1] + 1) * NORM_ROWS <= seq - LAG * L:
            mine, tail = tail[:per_group], tail[per_group:]
        else:
            while tail:
                tail.pop(0)()
            mine = []
        finish([(t, pr) for t in g for pr in range(pairs)], mine)


def _rwkv(p3, plora, prm, wup, *, batch, seq):
    pairs = RWKV_PAIRS_PER_STEP
    groups = RWKV_WIDTH // LANES // pairs
    n_chunks = seq // CHUNK
    assert n_chunks % LAG == 0
    assert seq % (NORM_ROWS * NORM_UNROLL) == 0

    def slab(base):
        return pl.BlockSpec((pairs, seq, LANES), lambda b, h: (base // pairs + h, b, 0))

    def per_pair(rows, dtype):
        return pltpu.VMEM((pairs, rows, LANES), dtype)

    return pl.pallas_call(
        functools.partial(_rwkv_kernel, seq=seq, pairs=pairs),
        grid=(batch, groups),
        in_specs=[
            slab(SLAB_R), slab(SLAB_K), slab(SLAB_V), slab(SLAB_ZA),
            pl.BlockSpec((seq, LANES), lambda b, h: (b, 0)),
            pl.BlockSpec((pairs, 16, LANES), lambda b, h: (h, 0, 0)),
            pl.BlockSpec((pairs, LANES, 2 * LANES), lambda b, h: (h, 0, 0)),
        ],
        out_specs=pl.BlockSpec((seq, pairs * LANES), lambda b, h: (b, h)),
        out_shape=jax.ShapeDtypeStruct((batch * seq, RWKV_WIDTH), BF16),
        scratch_shapes=[
            per_pair((n_chunks + LAG) * LANES, BF16), per_pair((n_chunks + LAG) * LANES, BF16),
            per_pair(seq + LAG * CHUNK, F32), per_pair(seq + LAG * CHUNK, F32),
            per_pair(seq + LAG * CHUNK, F32), per_pair(seq, F32),
            pltpu.VMEM((3 * pairs + 1, seq + SHIFT_PAD, LANES), F32),
        ],
        compiler_params=pltpu.CompilerParams(
            dimension_semantics=("arbitrary", "arbitrary"), vmem_limit_bytes=VMEM_LIMIT),
        name="rwkv7_mix",
    )(p3, p3, p3, p3, plora, prm, wup)


def _moba_kernel(q_ref, k_ref, v_ref, z_ref, qw_ref, kw_ref, o_ref,
                 q0_ref, q1_ref, ks_ref, vt_ref, gate_ref, *, seq):
    blk = MOBA_BLOCK
    nb = seq // blk
    ones_bd = _head_ones()
    first = _lane_iota((seq, LANES)) < HEAD_DIM

    def normed(ref, w_ref):
        x = ref[...]
        ms = _head_sum(x * x, ones_bd) * (1.0 / HEAD_DIM)
        return x * lax.rsqrt(ms + RMS_EPS) * w_ref[...]

    qn = normed(q_ref, qw_ref)
    kn = normed(k_ref, kw_ref)
    qs = qn * (HEAD_DIM ** -0.5 * LOG2_E)
    q0_ref[...] = jnp.where(first, qs, 0.0).astype(BF16)
    q1_ref[...] = jnp.where(first, 0.0, qs).astype(BF16)
    ks_ref[...] = kn.astype(BF16)
    eye = jnp.where(_lane_iota((LANES, LANES)) == _row_iota((LANES, LANES)), 1.0, 0.0).astype(BF16)
    vt_ref[...] = _dot_nt(eye, v_ref[...].astype(BF16)).astype(BF16)

    km = jnp.mean(kn.reshape(nb, blk, LANES), axis=1)
    first_nb = _lane_iota((nb, LANES)) < HEAD_DIM
    km_bd = jnp.concatenate([jnp.where(first_nb, km, 0.0), jnp.where(first_nb, 0.0, km)], axis=0)
    km_hi, km_lo = _split2(km_bd)
    q_hi, q_lo = _split2(qn)
    gate_ref[...] = _dot_nt(km_hi, q_hi) + _dot_nt(km_hi, q_lo) + _dot_nt(km_lo, q_hi)

    causal_t = _row_iota((blk, blk)) <= _lane_iota((blk, blk))
    blk_row = _row_iota((nb, blk))
    top_rows = _row_iota((LANES, blk)) < HEAD_DIM

    def scores(i, h):
        qh_ref = q0_ref if h == 0 else q1_ref
        return _dot_nt(ks_ref[0:(i + 1) * blk, :], qh_ref[i * blk:(i + 1) * blk, :])

    def attend(i, h, s):
        pieces = []
        if i > MOBA_TOPK:
            gate = gate_ref[h * nb:(h + 1) * nb, i * blk:(i + 1) * blk]
        for j in range(i):
            sj = s[j * blk:(j + 1) * blk, :]
            if i > MOBA_TOPK:
                gj = gate[j:j + 1, :]
                beats = (blk_row < i) & ((gate > gj) | ((gate == gj) & (blk_row < j)))
                rank = jnp.sum(jnp.where(beats, 1.0, 0.0), axis=0, keepdims=True)
                sj = jnp.where(rank < MOBA_TOPK, sj, NEG_INF)
            pieces.append(sj)
        pieces.append(jnp.where(causal_t, s[i * blk:, :], NEG_INF))
        s = jnp.concatenate(pieces, axis=0) if i > 0 else pieces[0]
        m = jnp.max(s, axis=0, keepdims=True)
        e = jnp.exp2(s - m)
        denom = jnp.sum(e, axis=0, keepdims=True)
        vt = vt_ref[h * HEAD_DIM:(h + 1) * HEAD_DIM, 0:(i + 1) * blk]
        return _dot(vt, e.astype(BF16)) * (1.0 / denom)

    units = [(i, h) for i in range(nb) for h in range(2)]
    s_next = scores(*units[0])
    outs = []
    for n, (i, h) in enumerate(units):
        s_cur = s_next
        if n + 1 < len(units):
            s_next = scores(*units[n + 1])
        outs.append(attend(i, h, s_cur))
        if h == 1:
            rows = slice(i * blk, (i + 1) * blk)
            o = jnp.concatenate([outs[-2], outs[-1]], axis=0).T
            z = z_ref[rows, :]
            o_ref[rows, :] = (o * (z / (1.0 + jnp.exp(-z)))).astype(o_ref.dtype)


def _moba(p3, qw, kw, *, batch, seq):
    hp = MOBA_WIDTH // LANES

    def slab(base):
        return pl.BlockSpec((None, seq, LANES), lambda b, h: (base + h, b, 0))

    return pl.pallas_call(
        functools.partial(_moba_kernel, seq=seq),
        grid=(batch, hp),
        in_specs=[
            slab(SLAB_Q), slab(SLAB_KQ), slab(SLAB_VQ), slab(SLAB_ZB),
            pl.BlockSpec((1, LANES), lambda b, h: (0, 0)),
            pl.BlockSpec((1, LANES), lambda b, h: (0, 0)),
        ],
        out_specs=pl.BlockSpec((seq, LANES), lambda b, h: (b, h)),
        out_shape=jax.ShapeDtypeStruct((batch * seq, MOBA_WIDTH), BF16),
        scratch_shapes=[
            pltpu.VMEM((seq, LANES), BF16), pltpu.VMEM((seq, LANES), BF16), pltpu.VMEM((seq, LANES), BF16),
            pltpu.VMEM((LANES, seq), BF16), pltpu.VMEM((2 * (seq // MOBA_BLOCK), seq), F32),
        ],
        compiler_params=pltpu.CompilerParams(
            dimension_semantics=("arbitrary", "arbitrary"), vmem_limit_bytes=VMEM_LIMIT),
        name="moba_attention",
    )(p3, p3, p3, p3, qw, kw)


def _out_kernel(x_ref, ya_ref, yb_ref, g_ref, wa_ref, wb_ref, wo_ref, o_ref, *, d_model):
    n = d_model // LANES
    pa = _dot(ya_ref[...], wa_ref[...])
    pb = _dot(yb_ref[...], wb_ref[...])
    ga = jnp.concatenate([g_ref[c] for c in range(n)], axis=1)
    gb = jnp.concatenate([g_ref[n + c] for c in range(n)], axis=1)
    merged = pa / (1.0 + jnp.exp(-ga)) + pb / (1.0 + jnp.exp(-gb))
    o_ref[...] = x_ref[...] + _dot(merged.astype(BF16), wo_ref[...])


def _out_proj(x2, ya, yb, p3, wa, wb, wo, *, tm):
    m, d = x2.shape
    n_g = 2 * d // LANES
    const = dict(pipeline_mode=pl.Buffered(1))
    return pl.pallas_call(
        functools.partial(_out_kernel, d_model=d),
        grid=(m // tm,),
        in_specs=[
            pl.BlockSpec((tm, d), lambda i: (i, 0)),
            pl.BlockSpec((tm, RWKV_WIDTH), lambda i: (i, 0)),
            pl.BlockSpec((tm, MOBA_WIDTH), lambda i: (i, 0)),
            pl.BlockSpec((n_g, tm, LANES), lambda i: (SLAB_G // n_g, i, 0)),
            pl.BlockSpec((RWKV_WIDTH, d), lambda i: (0, 0), **const),
            pl.BlockSpec((MOBA_WIDTH, d), lambda i: (0, 0), **const),
            pl.BlockSpec((d, d), lambda i: (0, 0), **const),
        ],
        out_specs=pl.BlockSpec((tm, d), lambda i: (i, 0)),
        out_shape=jax.ShapeDtypeStruct((m, d), F32),
        compiler_params=pltpu.CompilerParams(
            dimension_semantics=("arbitrary",), vmem_limit_bytes=VMEM_LIMIT),
        name="merge_out_proj",
    )(x2, ya, yb, p3, wa, wb, wo)


def _layer(x2, batch, seq, norm_w, w_in, mu_r, mu_k, mu_v, mu_w, mu_a, w0, w_decay_up, a0, w_iclr_up,
           k_k, k_a, r_k, gn_w, gn_b, q_norm_w, k_norm_w, w_proj_rwkv, w_proj_moba, w_out):
    d = x2.shape[1]
    lora_at = 4 * RWKV_WIDTH
    w_main = jnp.concatenate([w_in[:, :lora_at], w_in[:, lora_at + 2 * LORA:]], axis=1).astype(BF16)
    w_lora = w_in[:, lora_at:lora_at + 2 * LORA].astype(BF16)
    assert w_main.shape[1] == N_SLABS * LANES and d == (N_SLABS * LANES - SLAB_G * LANES) // 2

    p3, plora = _in_proj(x2, norm_w.reshape(1, d), w_main, w_lora, tm=min(1024, x2.shape[0]), tn=512)

    hp = RWKV_WIDTH // LANES
    vecs = jnp.stack([mu_r, mu_k, mu_v, w0, a0, k_k, k_a, r_k, gn_w, gn_b]).reshape(10, hp, LANES)
    mu_l = jnp.broadcast_to(jnp.concatenate([mu_w, mu_a]).reshape(1, 1, LANES), (1, hp, LANES))
    prm = jnp.concatenate([vecs, mu_l, jnp.zeros((5, hp, LANES), F32)], axis=0).transpose(1, 0, 2)
    zeros = jnp.zeros((hp, LORA, LANES), F32)
    wd = w_decay_up.reshape(LORA, hp, LANES).transpose(1, 0, 2)
    wa_up = w_iclr_up.reshape(LORA, hp, LANES).transpose(1, 0, 2)
    wup = jnp.concatenate([jnp.concatenate([wd, zeros], axis=2),
                           jnp.concatenate([zeros, wa_up], axis=2)], axis=1)

    ya = _rwkv(p3, plora, prm, wup, batch=batch, seq=seq)
    qw = jnp.tile(q_norm_w, 2).reshape(1, LANES)
    kw = jnp.tile(k_norm_w, 2).reshape(1, LANES)
    yb = _moba(p3, qw, kw, batch=batch, seq=seq)
    return _out_proj(x2, ya, yb, p3, w_proj_rwkv.astype(BF16), w_proj_moba.astype(BF16),
                     w_out.astype(BF16), tm=256)


def kernel(x, norm_w, w_in, mu_r, mu_k, mu_v, mu_w, mu_a, w0, w_decay_up, a0, w_iclr_up, k_k, k_a, r_k,
           gn_w, gn_b, q_norm_w, k_norm_w, w_proj_rwkv, w_proj_moba, w_out):
    batch, seq, d = x.shape
    assert seq % MOBA_BLOCK == 0 and seq % CHUNK == 0
    params = (norm_w, w_in, mu_r, mu_k, mu_v, mu_w, mu_a, w0, w_decay_up, a0, w_iclr_up, k_k, k_a, r_k,
              gn_w, gn_b, q_norm_w, k_norm_w, w_proj_rwkv, w_proj_moba, w_out)
    x2 = x.reshape(batch * seq, d)
    for layer in range(norm_w.shape[0]):
        x2 = _layer(x2, batch, seq, *[p[layer] for p in params])
    return x2.reshape(batch, seq, d)
```

```python
import functools

import jax
import jax.numpy as jnp
from jax import lax
from jax.experimental import pallas as pl
from jax.experimental.pallas import tpu as pltpu

F32 = jnp.float32
BF16 = jnp.bfloat16

LANES = 128
HEAD_DIM = 64
RWKV_WIDTH = 1024
MOBA_WIDTH = 1024
LORA = 64
MOBA_BLOCK = 256
MOBA_TOPK = 3
RMS_EPS = 1e-6
GN_EPS = 64e-5
NEG_INF = -1e30
LOG2_E = 1.4426950408889634
CHUNK = 64
VMEM_LIMIT = 56 * 1024 * 1024
RWKV_PAIRS_PER_STEP = 2
LAG = 8
NORM_ROWS = 256
NORM_UNROLL = 2
SCORE_LOOKAHEAD = 3
SHIFT_PAD = 8

SLAB_R, SLAB_K, SLAB_V, SLAB_ZA = 0, 8, 16, 24
SLAB_Q, SLAB_KQ, SLAB_VQ, SLAB_ZB = 32, 40, 48, 56
SLAB_G = 64
N_SLABS = 96


def _split2(x):
    hi = x.astype(BF16)
    lo = (x - hi.astype(F32)).astype(BF16)
    return hi, lo


def _split_lanes(x):
    hi, lo = _split2(x)
    return jnp.concatenate([hi, lo], axis=1)


def _dot(a, b):
    return jnp.dot(a, b, preferred_element_type=F32)


def _dot_nt(a, b):
    return lax.dot_general(a, b, (((1,), (1,)), ((), ())), preferred_element_type=F32)


def _dot_tn(a, b):
    return lax.dot_general(a, b, (((0,), (0,)), ((), ())), preferred_element_type=F32)


def _lane_iota(shape):
    return lax.broadcasted_iota(jnp.int32, shape, len(shape) - 1)


def _row_iota(shape):
    return lax.broadcasted_iota(jnp.int32, shape, 0)


def _head_ones():
    r = _row_iota((2 * LANES, LANES)) & (LANES - 1)
    c = _lane_iota((2 * LANES, LANES))
    return jnp.where((r < HEAD_DIM) == (c < HEAD_DIM), 1.0, 0.0).astype(BF16)


def _head_sum(x, ones2):
    return _dot(_split_lanes(x), ones2)


def _bd(x):
    first = _lane_iota(x.shape) < HEAD_DIM
    zero = jnp.zeros_like(x)
    return jnp.concatenate([jnp.where(first, x, zero), jnp.where(first, zero, x)], axis=0)


def _fold(m):
    first = _lane_iota((HEAD_DIM, LANES)) < HEAD_DIM
    return jnp.where(first, m[:HEAD_DIM], m[HEAD_DIM:])


def _in_proj_kernel(x_ref, nw_ref, w_ref, wl_ref, p_ref, pl_ref, h_ref, *, slabs):
    @pl.when(pl.program_id(1) == 0)
    def _():
        x = x_ref[...]
        ms = jnp.mean(x * x, axis=-1, keepdims=True)
        h = ((x * lax.rsqrt(ms + RMS_EPS)) * nw_ref[...]).astype(BF16)
        h_ref[...] = h
        pl_ref[...] = _dot(h, wl_ref[...].astype(BF16))

    acc = _dot(h_ref[...], w_ref[...].astype(BF16))
    for c in range(slabs):
        p_ref[c] = acc[:, c * LANES:(c + 1) * LANES]


def _in_proj(x2, norm_w, w_in, *, tm, tn):
    m, d = x2.shape
    lora_at = 4 * RWKV_WIDTH
    n = w_in.shape[1] - 2 * LORA
    slabs = tn // LANES
    assert n == N_SLABS * LANES and lora_at % tn == 0 and n % tn == 0 and 2 * LORA == LANES
    whole = pl.Element(d)
    return pl.pallas_call(
        functools.partial(_in_proj_kernel, slabs=slabs),
        grid=(m // tm, n // tn),
        in_specs=[
            pl.BlockSpec((tm, d), lambda i, j: (i, 0)),
            pl.BlockSpec((1, d), lambda i, j: (0, 0)),
            pl.BlockSpec((whole, pl.Element(tn)),
                         lambda i, j: (0, (j * slabs + jnp.where(j * tn >= lora_at, 1, 0)) * LANES)),
            pl.BlockSpec((whole, pl.Element(LANES)), lambda i, j: (0, lora_at)),
        ],
        out_specs=[
            pl.BlockSpec((slabs, tm, LANES), lambda i, j: (j, i, 0)),
            pl.BlockSpec((tm, LANES), lambda i, j: (i, 0)),
        ],
        out_shape=[
            jax.ShapeDtypeStruct((n // LANES, m, LANES), F32),
            jax.ShapeDtypeStruct((m, LANES), F32),
        ],
        scratch_shapes=[pltpu.VMEM((tm, d), BF16)],
        compiler_params=pltpu.CompilerParams(
            dimension_semantics=("arbitrary", "arbitrary"), vmem_limit_bytes=VMEM_LIMIT),
        name="in_proj",
    )(x2, norm_w, w_in, w_in)


def _mm(a, b):
    return _dot(a.astype(BF16), _bd(b).astype(BF16))


def _mm_nt(a, b):
    return _dot_nt(a.astype(BF16), _bd(b).astype(BF16))


def _mm_tn(a, b):
    return _fold(_dot_tn(a.astype(BF16), b.astype(BF16)))


def _rwkv_kernel(r_ref, k_ref, v_ref, z_ref, lo_ref, prm_ref, wup_ref, o_ref,
                 whi_ref, wlo_ref, hadd_ref, rp_ref, y0_ref, bonus_ref, pad_ref, *, seq, pairs):
    n_chunks = seq // CHUNK
    L = CHUNK
    ones_bd = _head_ones()
    row = _row_iota((L, LANES))
    lane = _lane_iota((L, LANES))
    col = lane & (HEAD_DIM - 1)
    strict = col < row
    incl = col <= row
    eye = col == row
    first_lanes = lane < HEAD_DIM
    tri = jnp.where(_lane_iota((L, L)) <= _row_iota((L, L)), 1.0, 0.0).astype(BF16)
    tri2 = jnp.concatenate([tri, tri], axis=1)
    row2 = _row_iota((LANES, LANES))
    lane2 = _lane_iota((LANES, LANES))
    eye2 = row2 == lane2
    same_head = (row2 < HEAD_DIM) == (lane2 < HEAD_DIM)

    shift_srcs = [ref.at[pr] for pr in range(pairs) for ref in (r_ref, k_ref, v_ref)] + [lo_ref]
    for n, src in enumerate(shift_srcs):
        pad_ref[n, 0:SHIFT_PAD, :] = jnp.zeros((SHIFT_PAD, LANES), F32)

    def copy_rows(t, carry):
        rows = pl.ds(pl.multiple_of(t * NORM_ROWS, NORM_ROWS), NORM_ROWS)
        dst = pl.ds(pl.multiple_of(t * NORM_ROWS + SHIFT_PAD, SHIFT_PAD), NORM_ROWS)
        for n, src in enumerate(shift_srcs):
            pad_ref[n, dst, :] = src[rows, :]
        return carry

    lax.fori_loop(0, seq // NORM_ROWS, copy_rows, 0)

    def shifted(n, c, mu):
        cur = shift_srcs[n][pl.ds(pl.multiple_of(c * L, L), L), :]
        prev = pad_ref[n, pl.ds(c * L + (SHIFT_PAD - 1), L), :]
        return cur + (prev - cur) * mu

    def aligned(x, m):
        return x if isinstance(x, int) else pl.multiple_of(x, m)

    def prepare(chains, hooks):
        hooks = list(hooks)

        def tick():
            if hooks:
                hooks.pop(0)()

        def each(fn, *lists):
            return [fn(*args) for args in zip(*lists)]

        prms = [prm_ref[pr] for _, pr in chains]
        rows = [pl.ds(pl.multiple_of(c * L, L), L) for c, _ in chains]
        slot_rows = [pl.ds(pl.multiple_of((c + LAG) * L, L), L) for c, _ in chains]
        r = [shifted(3 * pr, c, prm[0:1]) for (c, pr), prm in zip(chains, prms)]
        k = [shifted(3 * pr + 1, c, prm[1:2]) for (c, pr), prm in zip(chains, prms)]
        v = [shifted(3 * pr + 2, c, prm[2:3]) for (c, pr), prm in zip(chains, prms)]
        xl = [shifted(3 * pairs, c, prm[10:11]) for (c, pr), prm in zip(chains, prms)]

        feat = each(lambda x: _split_lanes(jnp.where(first_lanes, jnp.tanh(x), x)), xl)
        wups = [_split2(wup_ref[pr]) for _, pr in chains]
        up = each(lambda f, w: _dot(f, jnp.concatenate([w[0], w[0]], axis=0)) + _dot(f[:, :LANES], w[1]),
                  feat, wups)
        tick()

        def decay_log(u, prm):
            neg = -(prm[3:4] + u[:, :LANES])
            softplus = jnp.maximum(neg, 0.0) + jnp.log(1.0 + jnp.exp(-jnp.abs(neg)))
            return -jnp.exp(-softplus - 0.5)

        logw = each(decay_log, up, prms)
        a = each(lambda u, prm: 1.0 / (1.0 + jnp.exp(-(prm[4:5] + u[:, LANES:]))), up, prms)

        kk = each(lambda k_, prm: k_ * prm[5:6], k, prms)
        ksq = each(lambda x: _head_sum(x * x, ones_bd), kk)
        kk = each(lambda x, s: x / jnp.maximum(jnp.sqrt(s), 1e-12), kk, ksq)
        k2 = each(lambda k_, a_, prm: k_ * (1.0 + (a_ - 1.0) * prm[6:7]), k, a, prms)
        b = each(lambda x, a_: x * a_, kk, a)
        rk = each(lambda r_, k_, prm: _head_sum(r_ * k_ * prm[7:8], ones_bd), r, k2, prms)
        for (c, pr), rw, x, v_ in zip(chains, rows, rk, v):
            bonus_ref[pr, rw, :] = x * v_
        tick()

        cum = each(lambda x: _dot(tri2, jnp.concatenate(_split2(x), axis=0)), logw)
        tick()
        cum_l = each(lambda x: x[L - 1:L, :], cum)
        g_inv = each(lambda x: jnp.exp(-x), cum)
        g_rest = each(lambda x, xl_: jnp.exp(xl_ - x), cum, cum_l)
        at = each(lambda kk_, x, lw: -kk_ * jnp.exp(x - lw), kk, cum, logw)
        rt = each(lambda r_, x: r_ * jnp.exp(x), r, cum)
        bt = each(lambda b_, g: b_ * g, b, g_inv)
        kt = each(lambda k_, g: k_ * g, k2, g_inv)
        bh = each(lambda b_, g: b_ * g, b, g_rest)
        kh = each(lambda k_, g: k_ * g, k2, g_rest)

        lhs = each(lambda x, y: jnp.concatenate([x, y], axis=0), at, rt)
        ab = each(_mm_nt, lhs, bt)
        ak = each(_mm_nt, lhs, kt)
        tick()
        zero = jnp.zeros((L, LANES), F32)
        n_ab = each(lambda x: jnp.where(strict, x[:L], zero), ab)
        n_ak = each(lambda x: jnp.where(strict, x[:L], zero), ak)
        a_rb = each(lambda x: jnp.where(incl, x[L:], zero), ab)
        a_rk = each(lambda x: jnp.where(incl, x[L:], zero), ak)

        t = each(lambda x: jnp.where(eye, 1.0, 0.0) + x, n_ab)
        pw = each(_mm, n_ab, n_ab)
        steps = CHUNK.bit_length() - 2
        for s in range(steps):
            t = each(lambda t_, p_: t_ + _mm(t_, p_), t, pw)
            if s + 1 < steps:
                pw = each(_mm, pw, pw)
            if s % 2 == 0:
                tick()

        akv = each(_mm, n_ak, v)
        p = each(_mm, t, at)
        q = each(_mm, t, akv)
        tick()
        rp = each(lambda x, m, p_: x + _mm(m, p_), rt, a_rb, p)
        y0 = each(lambda m1, q_, m2, v_: _dot(jnp.concatenate([m1, m2], axis=1).astype(BF16),
                                              jnp.concatenate([_bd(q_), _bd(v_)], axis=0).astype(BF16)),
                  a_rb, q, a_rk, v)
        hadd = each(lambda q_, b_, v_, k_: _fold(_dot_tn(jnp.concatenate([q_, v_], axis=0).astype(BF16),
                                                         jnp.concatenate([b_, k_], axis=0).astype(BF16))),
                    q, bh, v, kh)
        w = each(lambda xl_, p_, b_: jnp.where(eye2, jnp.exp(xl_), 0.0)
                 + jnp.where(same_head, _dot_tn(p_.astype(BF16), b_.astype(BF16)), 0.0), cum_l, p, bh)
        while hooks:
            tick()
        for i, (c, pr) in enumerate(chains):
            rp_ref[pr, slot_rows[i], :] = rp[i]
            y0_ref[pr, slot_rows[i], :] = y0[i]
            hadd_ref[pr, slot_rows[i], :] = hadd[i]
            w_hi, w_lo = _split2(w[i])
            wrows = pl.ds(pl.multiple_of((c + LAG) * LANES, LANES), LANES)
            whi_ref[pr, wrows, :] = w_hi
            wlo_ref[pr, wrows, :] = w_lo

    def advance(slot, states):
        rows = pl.ds(aligned(slot * L, L), L)
        wrows = pl.ds(aligned(slot * LANES, LANES), LANES)
        split = [_split2(s) for s in states]
        new = []
        for pr in range(pairs):
            s_hi, s_lo = split[pr]
            w_hi = whi_ref[pr, wrows, :]
            new.append(_dot(s_hi, w_hi) + _dot(s_lo, w_hi) + _dot(s_hi, wlo_ref[pr, wrows, :])
                       + hadd_ref[pr, rows, :])
        for pr in range(pairs):
            y0_ref[pr, rows, :] = _mm_nt(rp_ref[pr, rows, :], states[pr]) + y0_ref[pr, rows, :]
        return tuple(new)

    def finish(tiles, hooks):
        hooks = list(hooks)

        def tick():
            if hooks:
                hooks.pop(0)()

        rows = [pl.ds(t * NORM_ROWS, NORM_ROWS) for t, _ in tiles]
        y = [y0_ref[pr, pl.ds(t * NORM_ROWS + LAG * L, NORM_ROWS), :] for t, pr in tiles]
        mean = [_head_sum(x, ones_bd) * (1.0 / HEAD_DIM) for x in y]
        tick()
        yc = [x - m for x, m in zip(y, mean)]
        var = [_head_sum(x * x, ones_bd) * (1.0 / HEAD_DIM) for x in yc]
        tick()
        for (_, pr), rw, x, s in zip(tiles, rows, yc, var):
            prm = prm_ref[pr]
            yn = x * lax.rsqrt(s + GN_EPS) * prm[8:9] + prm[9:10] + bonus_ref[pr, rw, :]
            z = z_ref[pr, rw, :]
            o_ref[rw, pr * LANES:(pr + 1) * LANES] = (yn * (z / (1.0 + jnp.exp(-z)))).astype(o_ref.dtype)
        while hooks:
            tick()

    for pr in range(pairs):
        whi_ref[pr, 0:LAG * LANES, :] = jnp.zeros((LAG * LANES, LANES), BF16)
        wlo_ref[pr, 0:LAG * LANES, :] = jnp.zeros((LAG * LANES, LANES), BF16)
        for ref in (hadd_ref, rp_ref, y0_ref):
            ref[pr, 0:LAG * L, :] = jnp.zeros((LAG * L, LANES), F32)

    def phase1(i, states):
        states = list(states)

        def step(u):
            def hook():
                states[:] = advance(i * LAG + u, tuple(states))
            return hook

        prepare([(i * LAG + u, pr) for u in range(LAG) for pr in range(pairs)],
                [step(u) for u in range(LAG)])
        return tuple(states)

    states = lax.fori_loop(0, n_chunks // LAG, phase1,
                           tuple(jnp.zeros((HEAD_DIM, LANES), F32) for _ in range(pairs)))
    states = list(states)

    def tail_step(u):
        def hook():
            states[:] = advance(n_chunks + u, tuple(states))
        return hook

    tail = [tail_step(u) for u in range(LAG)]
    n_tiles = seq // NORM_ROWS
    groups = [list(range(g, g + NORM_UNROLL)) for g in range(0, n_tiles, NORM_UNROLL)]
    n_early = sum(1 for g in groups if (g[-1] + 1) * NORM_ROWS <= seq - LAG * L)
    per_group = -(-LAG // max(n_early, 1))
    for g in groups:
        if (g[-1] + 1) * NORM_ROWS <= seq - LAG * L:
            mine, tail = tail[:per_group], tail[per_group:]
        else:
            while tail:
                tail.pop(0)()
            mine = []
        finish([(t, pr) for t in g for pr in range(pairs)], mine)


def _rwkv(p3, plora, prm, wup, *, batch, seq):
    pairs = RWKV_PAIRS_PER_STEP
    groups = RWKV_WIDTH // LANES // pairs
    n_chunks = seq // CHUNK
    assert n_chunks % LAG == 0
    assert seq % (NORM_ROWS * NORM_UNROLL) == 0

    def slab(base):
        return pl.BlockSpec((pairs, seq, LANES), lambda b, h: (base // pairs + h, b, 0))

    def per_pair(rows, dtype):
        return pltpu.VMEM((pairs, rows, LANES), dtype)

    return pl.pallas_call(
        functools.partial(_rwkv_kernel, seq=seq, pairs=pairs),
        grid=(batch, groups),
        in_specs=[
            slab(SLAB_R), slab(SLAB_K), slab(SLAB_V), slab(SLAB_ZA),
            pl.BlockSpec((seq, LANES), lambda b, h: (b, 0)),
            pl.BlockSpec((pairs, 16, LANES), lambda b, h: (h, 0, 0)),
            pl.BlockSpec((pairs, LANES, 2 * LANES), lambda b, h: (h, 0, 0)),
        ],
        out_specs=pl.BlockSpec((seq, pairs * LANES), lambda b, h: (b, h)),
        out_shape=jax.ShapeDtypeStruct((batch * seq, RWKV_WIDTH), BF16),
        scratch_shapes=[
            per_pair((n_chunks + LAG) * LANES, BF16), per_pair((n_chunks + LAG) * LANES, BF16),
            per_pair(seq + LAG * CHUNK, F32), per_pair(seq + LAG * CHUNK, F32),
            per_pair(seq + LAG * CHUNK, F32), per_pair(seq, F32),
            pltpu.VMEM((3 * pairs + 1, seq + SHIFT_PAD, LANES), F32),
        ],
        compiler_params=pltpu.CompilerParams(
            dimension_semantics=("arbitrary", "arbitrary"), vmem_limit_bytes=VMEM_LIMIT),
        name="rwkv7_mix",
    )(p3, p3, p3, p3, plora, prm, wup)


def _moba_kernel(q_ref, k_ref, v_ref, z_ref, qw_ref, kw_ref, o_ref,
                 q0_ref, q1_ref, ks_ref, vt_ref, gate_ref, *, seq):
    blk = MOBA_BLOCK
    nb = seq // blk
    ones_bd = _head_ones()
    first = _lane_iota((seq, LANES)) < HEAD_DIM

    def normed(ref, w_ref):
        x = ref[...]
        ms = _head_sum(x * x, ones_bd) * (1.0 / HEAD_DIM)
        return x * lax.rsqrt(ms + RMS_EPS) * w_ref[...]

    qn = normed(q_ref, qw_ref)
    kn = normed(k_ref, kw_ref)
    qs = qn * (HEAD_DIM ** -0.5 * LOG2_E)
    q0_ref[...] = jnp.where(first, qs, 0.0).astype(BF16)
    q1_ref[...] = jnp.where(first, 0.0, qs).astype(BF16)
    ks_ref[...] = kn.astype(BF16)
    eye = jnp.where(_lane_iota((LANES, LANES)) == _row_iota((LANES, LANES)), 1.0, 0.0).astype(BF16)
    vt_ref[...] = _dot_nt(eye, v_ref[...].astype(BF16)).astype(BF16)

    km = jnp.mean(kn.reshape(nb, blk, LANES), axis=1)
    first_nb = _lane_iota((nb, LANES)) < HEAD_DIM
    km_bd = jnp.concatenate([jnp.where(first_nb, km, 0.0), jnp.where(first_nb, 0.0, km)], axis=0)
    km_hi, km_lo = _split2(km_bd)
    q_hi, q_lo = _split2(qn)
    gate_ref[...] = _dot_nt(km_hi, q_hi) + _dot_nt(km_hi, q_lo) + _dot_nt(km_lo, q_hi)

    causal_t = _row_iota((blk, blk)) <= _lane_iota((blk, blk))
    blk_row = _row_iota((nb, blk))
    top_rows = _row_iota((LANES, blk)) < HEAD_DIM

    def scores(i, h):
        qh_ref = q0_ref if h == 0 else q1_ref
        return _dot_nt(ks_ref[0:(i + 1) * blk, :], qh_ref[i * blk:(i + 1) * blk, :])

    def attend(i, h, s):
        pieces = []
        if i > MOBA_TOPK:
            gate = gate_ref[h * nb:(h + 1) * nb, i * blk:(i + 1) * blk]
        for j in range(i):
            sj = s[j * blk:(j + 1) * blk, :]
            if i > MOBA_TOPK:
                gj = gate[j:j + 1, :]
                beats = (blk_row < i) & ((gate > gj) | ((gate == gj) & (blk_row < j)))
                rank = jnp.sum(jnp.where(beats, 1.0, 0.0), axis=0, keepdims=True)
                sj = jnp.where(rank < MOBA_TOPK, sj, NEG_INF)
            pieces.append(sj)
        pieces.append(jnp.where(causal_t, s[i * blk:, :], NEG_INF))
        s = jnp.concatenate(pieces, axis=0) if i > 0 else pieces[0]
        m = jnp.max(s, axis=0, keepdims=True)
        e = jnp.exp2(s - m)
        denom = jnp.sum(e, axis=0, keepdims=True)
        vt = vt_ref[h * HEAD_DIM:(h + 1) * HEAD_DIM, 0:(i + 1) * blk]
        return _dot(vt, e.astype(BF16)) * (1.0 / denom)

    units = [(i, h) for i in range(nb) for h in range(2)]
    ahead = [scores(*u) for u in units[:SCORE_LOOKAHEAD]]
    outs = []
    for n, (i, h) in enumerate(units):
        s_cur = ahead.pop(0)
        if n + SCORE_LOOKAHEAD < len(units):
            ahead.append(scores(*units[n + SCORE_LOOKAHEAD]))
        outs.append(attend(i, h, s_cur))
        if h == 1:
            rows = slice(i * blk, (i + 1) * blk)
            o = jnp.concatenate([outs[-2], outs[-1]], axis=0).T
            z = z_ref[rows, :]
            o_ref[rows, :] = (o * (z / (1.0 + jnp.exp(-z)))).astype(o_ref.dtype)


def _moba(p3, qw, kw, *, batch, seq):
    hp = MOBA_WIDTH // LANES

    def slab(base):
        return pl.BlockSpec((None, seq, LANES), lambda b, h: (base + h, b, 0))

    return pl.pallas_call(
        functools.partial(_moba_kernel, seq=seq),
        grid=(batch, hp),
        in_specs=[
            slab(SLAB_Q), slab(SLAB_KQ), slab(SLAB_VQ), slab(SLAB_ZB),
            pl.BlockSpec((1, LANES), lambda b, h: (0, 0)),
            pl.BlockSpec((1, LANES), lambda b, h: (0, 0)),
        ],
        out_specs=pl.BlockSpec((seq, LANES), lambda b, h: (b, h)),
        out_shape=jax.ShapeDtypeStruct((batch * seq, MOBA_WIDTH), BF16),
        scratch_shapes=[
            pltpu.VMEM((seq, LANES), BF16), pltpu.VMEM((seq, LANES), BF16), pltpu.VMEM((seq, LANES), BF16),
            pltpu.VMEM((LANES, seq), BF16), pltpu.VMEM((2 * (seq // MOBA_BLOCK), seq), F32),
        ],
        compiler_params=pltpu.CompilerParams(
            dimension_semantics=("arbitrary", "arbitrary"), vmem_limit_bytes=VMEM_LIMIT),
        name="moba_attention",
    )(p3, p3, p3, p3, qw, kw)


def _out_kernel(x_ref, ya_ref, yb_ref, g_ref, wa_ref, wb_ref, wo_ref, o_ref, *, d_model):
    n = d_model // LANES
    pa = _dot(ya_ref[...], wa_ref[...])
    pb = _dot(yb_ref[...], wb_ref[...])
    ga = jnp.concatenate([g_ref[c] for c in range(n)], axis=1)
    gb = jnp.concatenate([g_ref[n + c] for c in range(n)], axis=1)
    merged = pa / (1.0 + jnp.exp(-ga)) + pb / (1.0 + jnp.exp(-gb))
    o_ref[...] = x_ref[...] + _dot(merged.astype(BF16), wo_ref[...])


def _out_proj(x2, ya, yb, p3, wa, wb, wo, *, tm):
    m, d = x2.shape
    n_g = 2 * d // LANES
    const = dict(pipeline_mode=pl.Buffered(1))
    return pl.pallas_call(
        functools.partial(_out_kernel, d_model=d),
        grid=(m // tm,),
        in_specs=[
            pl.BlockSpec((tm, d), lambda i: (i, 0)),
            pl.BlockSpec((tm, RWKV_WIDTH), lambda i: (i, 0)),
            pl.BlockSpec((tm, MOBA_WIDTH), lambda i: (i, 0)),
            pl.BlockSpec((n_g, tm, LANES), lambda i: (SLAB_G // n_g, i, 0)),
            pl.BlockSpec((RWKV_WIDTH, d), lambda i: (0, 0), **const),
            pl.BlockSpec((MOBA_WIDTH, d), lambda i: (0, 0), **const),
            pl.BlockSpec((d, d), lambda i: (0, 0), **const),
        ],
        out_specs=pl.BlockSpec((tm, d), lambda i: (i, 0)),
        out_shape=jax.ShapeDtypeStruct((m, d), F32),
        compiler_params=pltpu.CompilerParams(
            dimension_semantics=("arbitrary",), vmem_limit_bytes=VMEM_LIMIT),
        name="merge_out_proj",
    )(x2, ya, yb, p3, wa, wb, wo)


def _layer(x2, batch, seq, norm_w, w_in, mu_r, mu_k, mu_v, mu_w, mu_a, w0, w_decay_up, a0, w_iclr_up,
           k_k, k_a, r_k, gn_w, gn_b, q_norm_w, k_norm_w, w_proj_rwkv, w_proj_moba, w_out):
    d = x2.shape[1]
    assert d == (N_SLABS * LANES - SLAB_G * LANES) // 2
    p3, plora = _in_proj(x2, norm_w.reshape(1, d), w_in, tm=min(1024, x2.shape[0]), tn=512)

    hp = RWKV_WIDTH // LANES
    vecs = jnp.stack([mu_r, mu_k, mu_v, w0, a0, k_k, k_a, r_k, gn_w, gn_b]).reshape(10, hp, LANES)
    mu_l = jnp.broadcast_to(jnp.concatenate([mu_w, mu_a]).reshape(1, 1, LANES), (1, hp, LANES))
    prm = jnp.concatenate([vecs, mu_l, jnp.zeros((5, hp, LANES), F32)], axis=0).transpose(1, 0, 2)
    zeros = jnp.zeros((hp, LORA, LANES), F32)
    wd = w_decay_up.reshape(LORA, hp, LANES).transpose(1, 0, 2)
    wa_up = w_iclr_up.reshape(LORA, hp, LANES).transpose(1, 0, 2)
    wup = jnp.concatenate([jnp.concatenate([wd, zeros], axis=2),
                           jnp.concatenate([zeros, wa_up], axis=2)], axis=1)

    ya = _rwkv(p3, plora, prm, wup, batch=batch, seq=seq)
    qw = jnp.tile(q_norm_w, 2).reshape(1, LANES)
    kw = jnp.tile(k_norm_w, 2).reshape(1, LANES)
    yb = _moba(p3, qw, kw, batch=batch, seq=seq)
    return _out_proj(x2, ya, yb, p3, w_proj_rwkv.astype(BF16), w_proj_moba.astype(BF16),
                     w_out.astype(BF16), tm=256)


def kernel(x, norm_w, w_in, mu_r, mu_k, mu_v, mu_w, mu_a, w0, w_decay_up, a0, w_iclr_up, k_k, k_a, r_k,
           gn_w, gn_b, q_norm_w, k_norm_w, w_proj_rwkv, w_proj_moba, w_out):
    batch, seq, d = x.shape
    assert seq % MOBA_BLOCK == 0 and seq % CHUNK == 0
    params = (norm_w, w_in, mu_r, mu_k, mu_v, mu_w, mu_a, w0, w_decay_up, a0, w_iclr_up, k_k, k_a, r_k,
              gn_w, gn_b, q_norm_w, k_norm_w, w_proj_rwkv, w_proj_moba, w_out)
    x2 = x.reshape(batch * seq, d)
    for layer in range(norm_w.shape[0]):
        x2 = _layer(x2, batch, seq, *[p[layer] for p in params])
    return x2.reshape(batch, seq, d)
```

```python
import functools

import jax
import jax.numpy as jnp
from jax import lax
from jax.experimental import pallas as pl
from jax.experimental.pallas import tpu as pltpu

F32 = jnp.float32
BF16 = jnp.bfloat16

LANES = 128
HEAD_DIM = 64
RWKV_WIDTH = 1024
MOBA_WIDTH = 1024
LORA = 64
MOBA_BLOCK = 256
MOBA_TOPK = 3
RMS_EPS = 1e-6
GN_EPS = 64e-5
NEG_INF = -1e30
LOG2_E = 1.4426950408889634
CHUNK = 64
VMEM_LIMIT = 56 * 1024 * 1024
RWKV_PAIRS_PER_STEP = 2
LAG = 8
NORM_ROWS = 256
NORM_UNROLL = 2
SCORE_LOOKAHEAD = 3
ONES_ROWS = 16
SHIFT_PAD = 8

SLAB_R, SLAB_K, SLAB_V, SLAB_ZA = 0, 8, 16, 24
SLAB_Q, SLAB_KQ, SLAB_VQ, SLAB_ZB = 32, 40, 48, 56
SLAB_G = 64
N_SLABS = 96


def _split2(x):
    hi = x.astype(BF16)
    lo = (x - hi.astype(F32)).astype(BF16)
    return hi, lo


def _split_lanes(x):
    hi, lo = _split2(x)
    return jnp.concatenate([hi, lo], axis=1)


def _dot(a, b):
    return jnp.dot(a, b, preferred_element_type=F32)


def _dot_nt(a, b):
    return lax.dot_general(a, b, (((1,), (1,)), ((), ())), preferred_element_type=F32)


def _dot_tn(a, b):
    return lax.dot_general(a, b, (((0,), (0,)), ((), ())), preferred_element_type=F32)


def _lane_iota(shape):
    return lax.broadcasted_iota(jnp.int32, shape, len(shape) - 1)


def _row_iota(shape):
    return lax.broadcasted_iota(jnp.int32, shape, 0)


def _head_ones():
    r = _row_iota((2 * LANES, LANES)) & (LANES - 1)
    c = _lane_iota((2 * LANES, LANES))
    return jnp.where((r < HEAD_DIM) == (c < HEAD_DIM), 1.0, 0.0).astype(BF16)


def _head_sum(x, ones2):
    return _dot(_split_lanes(x), ones2)


def _bd(x):
    first = _lane_iota(x.shape) < HEAD_DIM
    zero = jnp.zeros_like(x)
    return jnp.concatenate([jnp.where(first, x, zero), jnp.where(first, zero, x)], axis=0)


def _fold(m):
    first = _lane_iota((HEAD_DIM, LANES)) < HEAD_DIM
    return jnp.where(first, m[:HEAD_DIM], m[HEAD_DIM:])


def _in_proj_kernel(x_ref, nw_ref, w_ref, wl_ref, p_ref, pl_ref, h_ref, *, slabs):
    @pl.when(pl.program_id(1) == 0)
    def _():
        x = x_ref[...]
        ms = jnp.mean(x * x, axis=-1, keepdims=True)
        h = ((x * lax.rsqrt(ms + RMS_EPS)) * nw_ref[...]).astype(BF16)
        h_ref[...] = h
        pl_ref[...] = _dot(h, wl_ref[...].astype(BF16))

    acc = _dot(h_ref[...], w_ref[...].astype(BF16))
    for c in range(slabs):
        p_ref[c] = acc[:, c * LANES:(c + 1) * LANES]


def _in_proj(x2, norm_w, w_in, *, tm, tn):
    m, d = x2.shape
    lora_at = 4 * RWKV_WIDTH
    n = w_in.shape[1] - 2 * LORA
    slabs = tn // LANES
    assert n == N_SLABS * LANES and lora_at % tn == 0 and n % tn == 0 and 2 * LORA == LANES
    whole = pl.Element(d)
    return pl.pallas_call(
        functools.partial(_in_proj_kernel, slabs=slabs),
        grid=(m // tm, n // tn),
        in_specs=[
            pl.BlockSpec((tm, d), lambda i, j: (i, 0), pipeline_mode=pl.Buffered(1)),
            pl.BlockSpec((1, d), lambda i, j: (0, 0)),
            pl.BlockSpec((whole, pl.Element(tn)),
                         lambda i, j: (0, (j * slabs + jnp.where(j * tn >= lora_at, 1, 0)) * LANES)),
            pl.BlockSpec((whole, pl.Element(LANES)), lambda i, j: (0, lora_at)),
        ],
        out_specs=[
            pl.BlockSpec((slabs, tm, LANES), lambda i, j: (j, i, 0)),
            pl.BlockSpec((tm, LANES), lambda i, j: (i, 0)),
        ],
        out_shape=[
            jax.ShapeDtypeStruct((n // LANES, m, LANES), F32),
            jax.ShapeDtypeStruct((m, LANES), F32),
        ],
        scratch_shapes=[pltpu.VMEM((tm, d), BF16)],
        compiler_params=pltpu.CompilerParams(
            dimension_semantics=("arbitrary", "arbitrary"), vmem_limit_bytes=VMEM_LIMIT),
        name="in_proj",
    )(x2, norm_w, w_in, w_in)


def _mm(a, b):
    return _dot(a.astype(BF16), _bd(b).astype(BF16))


def _mm_nt(a, b):
    return _dot_nt(a.astype(BF16), _bd(b).astype(BF16))


def _mm_tn(a, b):
    return _fold(_dot_tn(a.astype(BF16), b.astype(BF16)))


def _rwkv_kernel(r_ref, k_ref, v_ref, z_ref, lo_ref, prm_ref, wup_ref, o_ref,
                 whi_ref, wlo_ref, hadd_ref, rp_ref, y0_ref, bonus_ref, pad_ref, *, seq, pairs):
    n_chunks = seq // CHUNK
    L = CHUNK
    ones_bd = _head_ones()
    row = _row_iota((L, LANES))
    lane = _lane_iota((L, LANES))
    col = lane & (HEAD_DIM - 1)
    strict = col < row
    incl = col <= row
    eye = col == row
    first_lanes = lane < HEAD_DIM
    tri = jnp.where(_lane_iota((L, L)) <= _row_iota((L, L)), 1.0, 0.0).astype(BF16)
    tri2 = jnp.concatenate([tri, tri], axis=1)
    row2 = _row_iota((LANES, LANES))
    lane2 = _lane_iota((LANES, LANES))
    eye2 = row2 == lane2
    same_head = (row2 < HEAD_DIM) == (lane2 < HEAD_DIM)

    shift_srcs = [ref.at[pr] for pr in range(pairs) for ref in (r_ref, k_ref, v_ref)] + [lo_ref]
    for n, src in enumerate(shift_srcs):
        pad_ref[n, 0:SHIFT_PAD, :] = jnp.zeros((SHIFT_PAD, LANES), F32)

    def copy_rows(t, carry):
        rows = pl.ds(pl.multiple_of(t * NORM_ROWS, NORM_ROWS), NORM_ROWS)
        dst = pl.ds(pl.multiple_of(t * NORM_ROWS + SHIFT_PAD, SHIFT_PAD), NORM_ROWS)
        for n, src in enumerate(shift_srcs):
            pad_ref[n, dst, :] = src[rows, :]
        return carry

    lax.fori_loop(0, seq // NORM_ROWS, copy_rows, 0)

    def shifted(n, c, mu):
        cur = shift_srcs[n][pl.ds(pl.multiple_of(c * L, L), L), :]
        prev = pad_ref[n, pl.ds(c * L + (SHIFT_PAD - 1), L), :]
        return cur + (prev - cur) * mu

    def aligned(x, m):
        return x if isinstance(x, int) else pl.multiple_of(x, m)

    def prepare(chains, hooks):
        hooks = list(hooks)

        def tick():
            if hooks:
                hooks.pop(0)()

        def each(fn, *lists):
            return [fn(*args) for args in zip(*lists)]

        prms = [prm_ref[pr] for _, pr in chains]
        rows = [pl.ds(pl.multiple_of(c * L, L), L) for c, _ in chains]
        slot_rows = [pl.ds(pl.multiple_of((c + LAG) * L, L), L) for c, _ in chains]
        r = [shifted(3 * pr, c, prm[0:1]) for (c, pr), prm in zip(chains, prms)]
        k = [shifted(3 * pr + 1, c, prm[1:2]) for (c, pr), prm in zip(chains, prms)]
        v = [shifted(3 * pr + 2, c, prm[2:3]) for (c, pr), prm in zip(chains, prms)]
        xl = [shifted(3 * pairs, c, prm[10:11]) for (c, pr), prm in zip(chains, prms)]

        feat = each(lambda x: _split_lanes(jnp.where(first_lanes, jnp.tanh(x), x)), xl)
        wups = [_split2(wup_ref[pr]) for _, pr in chains]
        up = each(lambda f, w: _dot(f, jnp.concatenate([w[0], w[0]], axis=0)) + _dot(f[:, :LANES], w[1]),
                  feat, wups)
        tick()

        def decay_log(u, prm):
            neg = -(prm[3:4] + u[:, :LANES])
            softplus = jnp.maximum(neg, 0.0) + jnp.log(1.0 + jnp.exp(-jnp.abs(neg)))
            return -jnp.exp(-softplus - 0.5)

        logw = each(decay_log, up, prms)
        a = each(lambda u, prm: 1.0 / (1.0 + jnp.exp(-(prm[4:5] + u[:, LANES:]))), up, prms)

        kk = each(lambda k_, prm: k_ * prm[5:6], k, prms)
        ksq = each(lambda x: _head_sum(x * x, ones_bd), kk)
        kk = each(lambda x, s: x / jnp.maximum(jnp.sqrt(s), 1e-12), kk, ksq)
        k2 = each(lambda k_, a_, prm: k_ * (1.0 + (a_ - 1.0) * prm[6:7]), k, a, prms)
        b = each(lambda x, a_: x * a_, kk, a)
        rk = each(lambda r_, k_, prm: _head_sum(r_ * k_ * prm[7:8], ones_bd), r, k2, prms)
        for (c, pr), rw, x, v_ in zip(chains, rows, rk, v):
            bonus_ref[pr, rw, :] = x * v_
        tick()

        cum = each(lambda x: _dot(tri2, jnp.concatenate(_split2(x), axis=0)), logw)
        tick()
        cum_l = each(lambda x: x[L - 1:L, :], cum)
        g_inv = each(lambda x: jnp.exp(-x), cum)
        g_rest = each(lambda x, xl_: jnp.exp(xl_ - x), cum, cum_l)
        at = each(lambda kk_, x, lw: -kk_ * jnp.exp(x - lw), kk, cum, logw)
        rt = each(lambda r_, x: r_ * jnp.exp(x), r, cum)
        bt = each(lambda b_, g: b_ * g, b, g_inv)
        kt = each(lambda k_, g: k_ * g, k2, g_inv)
        bh = each(lambda b_, g: b_ * g, b, g_rest)
        kh = each(lambda k_, g: k_ * g, k2, g_rest)

        lhs = each(lambda x, y: jnp.concatenate([x, y], axis=0), at, rt)
        ab = each(_mm_nt, lhs, bt)
        ak = each(_mm_nt, lhs, kt)
        tick()
        zero = jnp.zeros((L, LANES), F32)
        n_ab = each(lambda x: jnp.where(strict, x[:L], zero), ab)
        n_ak = each(lambda x: jnp.where(strict, x[:L], zero), ak)
        a_rb = each(lambda x: jnp.where(incl, x[L:], zero), ab)
        a_rk = each(lambda x: jnp.where(incl, x[L:], zero), ak)

        t = each(lambda x: jnp.where(eye, 1.0, 0.0) + x, n_ab)
        pw = each(_mm, n_ab, n_ab)
        steps = CHUNK.bit_length() - 2
        for s in range(steps):
            t = each(lambda t_, p_: t_ + _mm(t_, p_), t, pw)
            if s + 1 < steps:
                pw = each(_mm, pw, pw)
            if s % 2 == 0:
                tick()

        akv = each(_mm, n_ak, v)
        p = each(_mm, t, at)
        q = each(_mm, t, akv)
        tick()
        rp = each(lambda x, m, p_: x + _mm(m, p_), rt, a_rb, p)
        y0 = each(lambda m1, q_, m2, v_: _dot(jnp.concatenate([m1, m2], axis=1).astype(BF16),
                                              jnp.concatenate([_bd(q_), _bd(v_)], axis=0).astype(BF16)),
                  a_rb, q, a_rk, v)
        hadd = each(lambda q_, b_, v_, k_: _fold(_dot_tn(jnp.concatenate([q_, v_], axis=0).astype(BF16),
                                                         jnp.concatenate([b_, k_], axis=0).astype(BF16))),
                    q, bh, v, kh)
        w = each(lambda xl_, p_, b_: jnp.where(eye2, jnp.exp(xl_), 0.0)
                 + jnp.where(same_head, _dot_tn(p_.astype(BF16), b_.astype(BF16)), 0.0), cum_l, p, bh)
        while hooks:
            tick()
        for i, (c, pr) in enumerate(chains):
            rp_ref[pr, slot_rows[i], :] = rp[i]
            y0_ref[pr, slot_rows[i], :] = y0[i]
            hadd_ref[pr, slot_rows[i], :] = hadd[i]
            w_hi, w_lo = _split2(w[i])
            wrows = pl.ds(pl.multiple_of((c + LAG) * LANES, LANES), LANES)
            whi_ref[pr, wrows, :] = w_hi
            wlo_ref[pr, wrows, :] = w_lo

    def advance(slot, states):
        rows = pl.ds(aligned(slot * L, L), L)
        wrows = pl.ds(aligned(slot * LANES, LANES), LANES)
        split = [_split2(s) for s in states]
        new = []
        for pr in range(pairs):
            s_hi, s_lo = split[pr]
            w_hi = whi_ref[pr, wrows, :]
            new.append(_dot(s_hi, w_hi) + _dot(s_lo, w_hi) + _dot(s_hi, wlo_ref[pr, wrows, :])
                       + hadd_ref[pr, rows, :])
        for pr in range(pairs):
            y0_ref[pr, rows, :] = _mm_nt(rp_ref[pr, rows, :], states[pr]) + y0_ref[pr, rows, :]
        return tuple(new)

    def finish(tiles, hooks):
        hooks = list(hooks)

        def tick():
            if hooks:
                hooks.pop(0)()

        rows = [pl.ds(t * NORM_ROWS, NORM_ROWS) for t, _ in tiles]
        y = [y0_ref[pr, pl.ds(t * NORM_ROWS + LAG * L, NORM_ROWS), :] for t, pr in tiles]
        mean = [_head_sum(x, ones_bd) * (1.0 / HEAD_DIM) for x in y]
        tick()
        yc = [x - m for x, m in zip(y, mean)]
        var = [_head_sum(x * x, ones_bd) * (1.0 / HEAD_DIM) for x in yc]
        tick()
        for (_, pr), rw, x, s in zip(tiles, rows, yc, var):
            prm = prm_ref[pr]
            yn = x * lax.rsqrt(s + GN_EPS) * prm[8:9] + prm[9:10] + bonus_ref[pr, rw, :]
            z = z_ref[pr, rw, :]
            o_ref[rw, pr * LANES:(pr + 1) * LANES] = (yn * (z / (1.0 + jnp.exp(-z)))).astype(o_ref.dtype)
        while hooks:
            tick()

    for pr in range(pairs):
        whi_ref[pr, 0:LAG * LANES, :] = jnp.zeros((LAG * LANES, LANES), BF16)
        wlo_ref[pr, 0:LAG * LANES, :] = jnp.zeros((LAG * LANES, LANES), BF16)
        for ref in (hadd_ref, rp_ref, y0_ref):
            ref[pr, 0:LAG * L, :] = jnp.zeros((LAG * L, LANES), F32)

    def phase1(i, states):
        states = list(states)

        def step(u):
            def hook():
                states[:] = advance(i * LAG + u, tuple(states))
            return hook

        prepare([(i * LAG + u, pr) for u in range(LAG) for pr in range(pairs)],
                [step(u) for u in range(LAG)])
        return tuple(states)

    states = lax.fori_loop(0, n_chunks // LAG, phase1,
                           tuple(jnp.zeros((HEAD_DIM, LANES), F32) for _ in range(pairs)))
    states = list(states)

    def tail_step(u):
        def hook():
            states[:] = advance(n_chunks + u, tuple(states))
        return hook

    tail = [tail_step(u) for u in range(LAG)]
    n_tiles = seq // NORM_ROWS
    groups = [list(range(g, g + NORM_UNROLL)) for g in range(0, n_tiles, NORM_UNROLL)]
    n_early = sum(1 for g in groups if (g[-1] + 1) * NORM_ROWS <= seq - LAG * L)
    per_group = -(-LAG // max(n_early, 1))
    for g in groups:
        if (g[-1] + 1) * NORM_ROWS <= seq - LAG * L:
            mine, tail = tail[:per_group], tail[per_group:]
        else:
            while tail:
                tail.pop(0)()
            mine = []
        finish([(t, pr) for t in g for pr in range(pairs)], mine)


def _rwkv(p3, plora, prm, wup, *, batch, seq):
    pairs = RWKV_PAIRS_PER_STEP
    groups = RWKV_WIDTH // LANES // pairs
    n_chunks = seq // CHUNK
    assert n_chunks % LAG == 0
    assert seq % (NORM_ROWS * NORM_UNROLL) == 0

    def slab(base):
        return pl.BlockSpec((pairs, seq, LANES), lambda b, h: (base // pairs + h, b, 0))

    def per_pair(rows, dtype):
        return pltpu.VMEM((pairs, rows, LANES), dtype)

    return pl.pallas_call(
        functools.partial(_rwkv_kernel, seq=seq, pairs=pairs),
        grid=(batch, groups),
        in_specs=[
            slab(SLAB_R), slab(SLAB_K), slab(SLAB_V), slab(SLAB_ZA),
            pl.BlockSpec((seq, LANES), lambda b, h: (b, 0)),
            pl.BlockSpec((pairs, 16, LANES), lambda b, h: (h, 0, 0)),
            pl.BlockSpec((pairs, LANES, 2 * LANES), lambda b, h: (h, 0, 0)),
        ],
        out_specs=pl.BlockSpec((seq, pairs * LANES), lambda b, h: (b, h)),
        out_shape=jax.ShapeDtypeStruct((batch * seq, RWKV_WIDTH), BF16),
        scratch_shapes=[
            per_pair((n_chunks + LAG) * LANES, BF16), per_pair((n_chunks + LAG) * LANES, BF16),
            per_pair(seq + LAG * CHUNK, F32), per_pair(seq + LAG * CHUNK, F32),
            per_pair(seq + LAG * CHUNK, F32), per_pair(seq, F32),
            pltpu.VMEM((3 * pairs + 1, seq + SHIFT_PAD, LANES), F32),
        ],
        compiler_params=pltpu.CompilerParams(
            dimension_semantics=("arbitrary", "arbitrary"), vmem_limit_bytes=VMEM_LIMIT),
        name="rwkv7_mix",
    )(p3, p3, p3, p3, plora, prm, wup)


def _moba_kernel(q_ref, k_ref, v_ref, z_ref, qw_ref, kw_ref, o_ref,
                 q0_ref, q1_ref, ks_ref, vt_ref, gate_ref, *, seq):
    blk = MOBA_BLOCK
    nb = seq // blk
    ones_bd = _head_ones()
    first = _lane_iota((seq, LANES)) < HEAD_DIM

    def normed(ref, w_ref):
        x = ref[...]
        ms = _head_sum(x * x, ones_bd) * (1.0 / HEAD_DIM)
        return x * lax.rsqrt(ms + RMS_EPS) * w_ref[...]

    qn = normed(q_ref, qw_ref)
    kn = normed(k_ref, kw_ref)
    qs = qn * (HEAD_DIM ** -0.5 * LOG2_E)
    q0_ref[...] = jnp.where(first, qs, 0.0).astype(BF16)
    q1_ref[...] = jnp.where(first, 0.0, qs).astype(BF16)
    ks_ref[...] = kn.astype(BF16)
    eye = jnp.where(_lane_iota((LANES, LANES)) == _row_iota((LANES, LANES)), 1.0, 0.0).astype(BF16)
    v_t = _dot_nt(eye, v_ref[...].astype(BF16)).astype(BF16)
    for h in range(2):
        vt_ref[h, 0:HEAD_DIM, :] = v_t[h * HEAD_DIM:(h + 1) * HEAD_DIM, :]
        vt_ref[h, HEAD_DIM:, :] = jnp.ones((ONES_ROWS, seq), BF16)

    km = jnp.mean(kn.reshape(nb, blk, LANES), axis=1)
    first_nb = _lane_iota((nb, LANES)) < HEAD_DIM
    km_bd = jnp.concatenate([jnp.where(first_nb, km, 0.0), jnp.where(first_nb, 0.0, km)], axis=0)
    km_hi, km_lo = _split2(km_bd)
    q_hi, q_lo = _split2(qn)
    gate_ref[...] = _dot_nt(km_hi, q_hi) + _dot_nt(km_hi, q_lo) + _dot_nt(km_lo, q_hi)

    causal_t = _row_iota((blk, blk)) <= _lane_iota((blk, blk))
    blk_row = _row_iota((nb, blk))
    top_rows = _row_iota((LANES, blk)) < HEAD_DIM

    def scores(i, h):
        qh_ref = q0_ref if h == 0 else q1_ref
        return _dot_nt(ks_ref[0:(i + 1) * blk, :], qh_ref[i * blk:(i + 1) * blk, :])

    def attend(i, h, s):
        pieces = []
        if i > MOBA_TOPK:
            gate = gate_ref[h * nb:(h + 1) * nb, i * blk:(i + 1) * blk]
        for j in range(i):
            sj = s[j * blk:(j + 1) * blk, :]
            if i > MOBA_TOPK:
                gj = gate[j:j + 1, :]
                beats = (blk_row < i) & ((gate > gj) | ((gate == gj) & (blk_row < j)))
                rank = jnp.sum(jnp.where(beats, 1.0, 0.0), axis=0, keepdims=True)
                sj = jnp.where(rank < MOBA_TOPK, sj, NEG_INF)
            pieces.append(sj)
        pieces.append(jnp.where(causal_t, s[i * blk:, :], NEG_INF))
        s = jnp.concatenate(pieces, axis=0) if i > 0 else pieces[0]
        m = jnp.max(s, axis=0, keepdims=True)
        e = jnp.exp2(s - m)
        pv = _dot(vt_ref[h, :, 0:(i + 1) * blk], e.astype(BF16))
        return pv[:HEAD_DIM] * (1.0 / pv[HEAD_DIM:HEAD_DIM + 1])

    units = [(i, h) for i in range(nb) for h in range(2)]
    ahead = [scores(*u) for u in units[:SCORE_LOOKAHEAD]]
    outs = []
    for n, (i, h) in enumerate(units):
        s_cur = ahead.pop(0)
        if n + SCORE_LOOKAHEAD < len(units):
            ahead.append(scores(*units[n + SCORE_LOOKAHEAD]))
        outs.append(attend(i, h, s_cur))
        if h == 1:
            rows = slice(i * blk, (i + 1) * blk)
            o = jnp.concatenate([outs[-2], outs[-1]], axis=0).T
            z = z_ref[rows, :]
            o_ref[rows, :] = (o * (z / (1.0 + jnp.exp(-z)))).astype(o_ref.dtype)


def _moba(p3, qw, kw, *, batch, seq):
    hp = MOBA_WIDTH // LANES

    def slab(base):
        return pl.BlockSpec((None, seq, LANES), lambda b, h: (base + h, b, 0))

    return pl.pallas_call(
        functools.partial(_moba_kernel, seq=seq),
        grid=(batch, hp),
        in_specs=[
            slab(SLAB_Q), slab(SLAB_KQ), slab(SLAB_VQ), slab(SLAB_ZB),
            pl.BlockSpec((1, LANES), lambda b, h: (0, 0)),
            pl.BlockSpec((1, LANES), lambda b, h: (0, 0)),
        ],
        out_specs=pl.BlockSpec((seq, LANES), lambda b, h: (b, h)),
        out_shape=jax.ShapeDtypeStruct((batch * seq, MOBA_WIDTH), BF16),
        scratch_shapes=[
            pltpu.VMEM((seq, LANES), BF16), pltpu.VMEM((seq, LANES), BF16), pltpu.VMEM((seq, LANES), BF16),
            pltpu.VMEM((2, HEAD_DIM + ONES_ROWS, seq), BF16),
            pltpu.VMEM((2 * (seq // MOBA_BLOCK), seq), F32),
        ],
        compiler_params=pltpu.CompilerParams(
            dimension_semantics=("arbitrary", "arbitrary"), vmem_limit_bytes=VMEM_LIMIT),
        name="moba_attention",
    )(p3, p3, p3, p3, qw, kw)


def _out_kernel(x_ref, ya_ref, yb_ref, g_ref, wa_ref, wb_ref, wo_ref, o_ref, *, d_model):
    n = d_model // LANES
    pa = _dot(ya_ref[...], wa_ref[...])
    pb = _dot(yb_ref[...], wb_ref[...])
    ga = jnp.concatenate([g_ref[c] for c in range(n)], axis=1)
    gb = jnp.concatenate([g_ref[n + c] for c in range(n)], axis=1)
    merged = pa / (1.0 + jnp.exp(-ga)) + pb / (1.0 + jnp.exp(-gb))
    o_ref[...] = x_ref[...] + _dot(merged.astype(BF16), wo_ref[...])


def _out_proj(x2, ya, yb, p3, wa, wb, wo, *, tm):
    m, d = x2.shape
    n_g = 2 * d // LANES
    const = dict(pipeline_mode=pl.Buffered(1))
    return pl.pallas_call(
        functools.partial(_out_kernel, d_model=d),
        grid=(m // tm,),
        in_specs=[
            pl.BlockSpec((tm, d), lambda i: (i, 0)),
            pl.BlockSpec((tm, RWKV_WIDTH), lambda i: (i, 0)),
            pl.BlockSpec((tm, MOBA_WIDTH), lambda i: (i, 0)),
            pl.BlockSpec((n_g, tm, LANES), lambda i: (SLAB_G // n_g, i, 0)),
            pl.BlockSpec((RWKV_WIDTH, d), lambda i: (0, 0), **const),
            pl.BlockSpec((MOBA_WIDTH, d), lambda i: (0, 0), **const),
            pl.BlockSpec((d, d), lambda i: (0, 0), **const),
        ],
        out_specs=pl.BlockSpec((tm, d), lambda i: (i, 0)),
        out_shape=jax.ShapeDtypeStruct((m, d), F32),
        compiler_params=pltpu.CompilerParams(
            dimension_semantics=("arbitrary",), vmem_limit_bytes=VMEM_LIMIT),
        name="merge_out_proj",
    )(x2, ya, yb, p3, wa, wb, wo)


def _layer(x2, batch, seq, norm_w, w_in, mu_r, mu_k, mu_v, mu_w, mu_a, w0, w_decay_up, a0, w_iclr_up,
           k_k, k_a, r_k, gn_w, gn_b, q_norm_w, k_norm_w, w_proj_rwkv, w_proj_moba, w_out):
    d = x2.shape[1]
    assert d == (N_SLABS * LANES - SLAB_G * LANES) // 2
    p3, plora = _in_proj(x2, norm_w.reshape(1, d), w_in, tm=min(2048, x2.shape[0]), tn=512)

    hp = RWKV_WIDTH // LANES
    vecs = jnp.stack([mu_r, mu_k, mu_v, w0, a0, k_k, k_a, r_k, gn_w, gn_b]).reshape(10, hp, LANES)
    mu_l = jnp.broadcast_to(jnp.concatenate([mu_w, mu_a]).reshape(1, 1, LANES), (1, hp, LANES))
    prm = jnp.concatenate([vecs, mu_l, jnp.zeros((5, hp, LANES), F32)], axis=0).transpose(1, 0, 2)
    zeros = jnp.zeros((hp, LORA, LANES), F32)
    wd = w_decay_up.reshape(LORA, hp, LANES).transpose(1, 0, 2)
    wa_up = w_iclr_up.reshape(LORA, hp, LANES).transpose(1, 0, 2)
    wup = jnp.concatenate([jnp.concatenate([wd, zeros], axis=2),
                           jnp.concatenate([zeros, wa_up], axis=2)], axis=1)

    ya = _rwkv(p3, plora, prm, wup, batch=batch, seq=seq)
    qw = jnp.tile(q_norm_w, 2).reshape(1, LANES)
    kw = jnp.tile(k_norm_w, 2).reshape(1, LANES)
    yb = _moba(p3, qw, kw, batch=batch, seq=seq)
    return _out_proj(x2, ya, yb, p3, w_proj_rwkv.astype(BF16), w_proj_moba.astype(BF16),
                     w_out.astype(BF16), tm=256)


def kernel(x, norm_w, w_in, mu_r, mu_k, mu_v, mu_w, mu_a, w0, w_decay_up, a0, w_iclr_up, k_k, k_a, r_k,
           gn_w, gn_b, q_norm_w, k_norm_w, w_proj_rwkv, w_proj_moba, w_out):
    batch, seq, d = x.shape
    assert seq % MOBA_BLOCK == 0 and seq % CHUNK == 0
    params = (norm_w, w_in, mu_r, mu_k, mu_v, mu_w, mu_a, w0, w_decay_up, a0, w_iclr_up, k_k, k_a, r_k,
              gn_w, gn_b, q_norm_w, k_norm_w, w_proj_rwkv, w_proj_moba, w_out)
    x2 = x.reshape(batch * seq, d)
    for layer in range(norm_w.shape[0]):
        x2 = _layer(x2, batch, seq, *[p[layer] for p in params])
    return x2.reshape(batch, seq, d)
```

```python
import functools

import jax
import jax.numpy as jnp
from jax import lax
from jax.experimental import pallas as pl
from jax.experimental.pallas import tpu as pltpu

F32 = jnp.float32
BF16 = jnp.bfloat16

LANES = 128
HEAD_DIM = 64
RWKV_WIDTH = 1024
MOBA_WIDTH = 1024
LORA = 64
MOBA_BLOCK = 256
MOBA_TOPK = 3
RMS_EPS = 1e-6
GN_EPS = 64e-5
NEG_INF = -1e30
LOG2_E = 1.4426950408889634
CHUNK = 64
VMEM_LIMIT = 56 * 1024 * 1024
RWKV_PAIRS_PER_STEP = 2
LAG = 8
B_TICKS = 10
A_TICKS = (0, 2, 4, 6)
NORM_ROWS = 256
NORM_UNROLL = 2
SCORE_LOOKAHEAD = 3
ONES_ROWS = 16

SLAB_R, SLAB_K, SLAB_V, SLAB_ZA = 0, 8, 16, 24
SLAB_Q, SLAB_KQ, SLAB_VQ, SLAB_ZB = 32, 40, 48, 56
SLAB_G = 64
N_SLABS = 96


def _split2(x):
    hi = x.astype(BF16)
    lo = (x - hi.astype(F32)).astype(BF16)
    return hi, lo


def _split_lanes(x):
    hi, lo = _split2(x)
    return jnp.concatenate([hi, lo], axis=1)


def _dot(a, b):
    return jnp.dot(a, b, preferred_element_type=F32)


def _dot_nt(a, b):
    return lax.dot_general(a, b, (((1,), (1,)), ((), ())), preferred_element_type=F32)


def _dot_tn(a, b):
    return lax.dot_general(a, b, (((0,), (0,)), ((), ())), preferred_element_type=F32)


def _lane_iota(shape):
    return lax.broadcasted_iota(jnp.int32, shape, len(shape) - 1)


def _row_iota(shape):
    return lax.broadcasted_iota(jnp.int32, shape, 0)


def _head_ones():
    r = _row_iota((2 * LANES, LANES)) & (LANES - 1)
    c = _lane_iota((2 * LANES, LANES))
    return jnp.where((r < HEAD_DIM) == (c < HEAD_DIM), 1.0, 0.0).astype(BF16)


def _head_sum(x, ones2):
    return _dot(_split_lanes(x), ones2)


def _bd(x):
    first = _lane_iota(x.shape) < HEAD_DIM
    zero = jnp.zeros_like(x)
    return jnp.concatenate([jnp.where(first, x, zero), jnp.where(first, zero, x)], axis=0)


def _fold(m):
    first = _lane_iota((HEAD_DIM, LANES)) < HEAD_DIM
    return jnp.where(first, m[:HEAD_DIM], m[HEAD_DIM:])


def _in_proj_kernel(x_ref, nw_ref, w_ref, wl_ref, p_ref, pl_ref, h_ref, *, slabs):
    @pl.when(pl.program_id(1) == 0)
    def _():
        x = x_ref[...]
        ms = jnp.mean(x * x, axis=-1, keepdims=True)
        h = ((x * lax.rsqrt(ms + RMS_EPS)) * nw_ref[...]).astype(BF16)
        h_ref[...] = h
        pl_ref[...] = _dot(h, wl_ref[...].astype(BF16))

    acc = _dot(h_ref[...], w_ref[...].astype(BF16))
    for c in range(slabs):
        p_ref[c] = acc[:, c * LANES:(c + 1) * LANES]


def _in_proj(x2, norm_w, w_in, *, tm, tn):
    m, d = x2.shape
    lora_at = 4 * RWKV_WIDTH
    n = w_in.shape[1] - 2 * LORA
    slabs = tn // LANES
    assert n == N_SLABS * LANES and lora_at % tn == 0 and n % tn == 0 and 2 * LORA == LANES
    whole = pl.Element(d)
    return pl.pallas_call(
        functools.partial(_in_proj_kernel, slabs=slabs),
        grid=(m // tm, n // tn),
        in_specs=[
            pl.BlockSpec((tm, d), lambda i, j: (i, 0), pipeline_mode=pl.Buffered(1)),
            pl.BlockSpec((1, d), lambda i, j: (0, 0)),
            pl.BlockSpec((whole, pl.Element(tn)),
                         lambda i, j: (0, (j * slabs + jnp.where(j * tn >= lora_at, 1, 0)) * LANES)),
            pl.BlockSpec((whole, pl.Element(LANES)), lambda i, j: (0, lora_at)),
        ],
        out_specs=[
            pl.BlockSpec((slabs, tm, LANES), lambda i, j: (j, i, 0)),
            pl.BlockSpec((tm, LANES), lambda i, j: (i, 0)),
        ],
        out_shape=[
            jax.ShapeDtypeStruct((n // LANES, m, LANES), F32),
            jax.ShapeDtypeStruct((m, LANES), F32),
        ],
        scratch_shapes=[pltpu.VMEM((tm, d), BF16)],
        compiler_params=pltpu.CompilerParams(
            dimension_semantics=("arbitrary", "arbitrary"), vmem_limit_bytes=VMEM_LIMIT),
        name="in_proj",
    )(x2, norm_w, w_in, w_in)


def _mm(a, b):
    return _dot(a.astype(BF16), _bd(b).astype(BF16))


def _rwkv_kernel(r_ref, k_ref, v_ref, z_ref, lo_ref, prm_ref, wup_ref, o_ref,
                 ghi_ref, glo_ref, hadd_ref, rp_ref, y0_ref, bonus_ref, sbf_ref, sf_ref,
                 *, seq, pairs):
    n_chunks = seq // CHUNK
    n_groups = n_chunks // LAG
    L = CHUNK
    n_chains = LAG * pairs
    ones_bd = _head_ones()
    row = _row_iota((L, LANES))
    lane = _lane_iota((L, LANES))
    col = lane & (HEAD_DIM - 1)
    strict = col < row
    incl = col <= row
    eye = col == row
    first_lanes = lane < HEAD_DIM
    tri = jnp.where(_lane_iota((L, L)) <= _row_iota((L, L)), 1.0, 0.0).astype(BF16)
    tri2 = jnp.concatenate([tri, tri], axis=1)

    def aligned(x, m):
        return x if isinstance(x, int) else pl.multiple_of(x, m)

    def shifted(ref, c, mu):
        cur = ref[pl.ds(aligned(c * L, L), L), :]
        if isinstance(c, int) and c == 0:
            prev = jnp.where(row == 0, 0.0, pltpu.roll(cur, 1, 0))
        else:
            prev = ref[pl.ds(c * L - 1, L), :]
        return cur + (prev - cur) * mu

    def each(fn, *lists):
        return [fn(*args) for args in zip(*lists)]

    def chains_of(group):
        return [(group * LAG + u, pr) for u in range(LAG) for pr in range(pairs)]

    def stage_a(group, parity):
        chains = chains_of(group)
        prms = [prm_ref[pr] for _, pr in chains]
        rows = [pl.ds(aligned(c * L, L), L) for c, _ in chains]
        r = [shifted(r_ref.at[pr], c, prm[0:1]) for (c, pr), prm in zip(chains, prms)]
        k = [shifted(k_ref.at[pr], c, prm[1:2]) for (c, pr), prm in zip(chains, prms)]
        v = [shifted(v_ref.at[pr], c, prm[2:3]) for (c, pr), prm in zip(chains, prms)]
        xl = [shifted(lo_ref, c, prm[10:11]) for (c, pr), prm in zip(chains, prms)]

        feat = each(lambda x: _split_lanes(jnp.where(first_lanes, jnp.tanh(x), x)), xl)
        wups = [_split2(wup_ref[pr]) for _, pr in chains]
        up = each(lambda f, w: _dot(f, jnp.concatenate([w[0], w[0]], axis=0)) + _dot(f[:, :LANES], w[1]),
                  feat, wups)
        yield

        def decay_log(u, prm):
            neg = -(prm[3:4] + u[:, :LANES])
            softplus = jnp.maximum(neg, 0.0) + jnp.log(1.0 + jnp.exp(-jnp.abs(neg)))
            return -jnp.exp(-softplus - 0.5)

        logw = each(decay_log, up, prms)
        a = each(lambda u, prm: 1.0 / (1.0 + jnp.exp(-(prm[4:5] + u[:, LANES:]))), up, prms)
        kk = each(lambda k_, prm: k_ * prm[5:6], k, prms)
        ksq = each(lambda x: _head_sum(x * x, ones_bd), kk)
        cum = each(lambda x: _dot(tri2, jnp.concatenate(_split2(x), axis=0)), logw)
        yield
        kk = each(lambda x, s: x / jnp.maximum(jnp.sqrt(s), 1e-12), kk, ksq)
        k2 = each(lambda k_, a_, prm: k_ * (1.0 + (a_ - 1.0) * prm[6:7]), k, a, prms)
        b = each(lambda x, a_: x * a_, kk, a)
        rk = each(lambda r_, k_, prm: _head_sum(r_ * k_ * prm[7:8], ones_bd), r, k2, prms)
        yield
        for (c, pr), rw, x, v_ in zip(chains, rows, rk, v):
            bonus_ref[pr, rw, :] = x * v_

        cum_l = each(lambda x: x[L - 1:L, :], cum)
        g_inv = each(lambda x: jnp.exp(-x), cum)
        g_rest = each(lambda x, xl_: jnp.exp(xl_ - x), cum, cum_l)
        staged = [
            each(lambda kk_, x, lw: -kk_ * jnp.exp(x - lw), kk, cum, logw),
            each(lambda b_, g: b_ * g, b, g_inv),
            each(lambda k_, g: k_ * g, k2, g_inv),
            each(lambda b_, g: b_ * g, b, g_rest),
            each(lambda k_, g: k_ * g, k2, g_rest),
            v,
        ]
        rt = each(lambda r_, x: r_ * jnp.exp(x), r, cum)
        for n in range(n_chains):
            for s, arrs in enumerate(staged):
                sbf_ref[parity, n, s] = arrs[n].astype(BF16)
            sf_ref[parity, n, 0:L, :] = rt[n]
            sf_ref[parity, n, L:L + 8, :] = jnp.broadcast_to(jnp.exp(cum_l[n]), (8, LANES))

    def stage_b(group, parity, hooks):
        hooks = list(hooks)

        def tick():
            if hooks:
                for hook in hooks.pop(0):
                    hook()

        chains = chains_of(group)
        idx = range(n_chains)
        at, bt, kt, bh, kh, v = ([sbf_ref[parity, n, s] for n in idx] for s in range(6))
        rt = [sf_ref[parity, n, 0:L, :] for n in idx]
        g_l = [sf_ref[parity, n, L:L + 1, :] for n in idx]

        def side(x, y):
            return jnp.concatenate([x, y], axis=1)

        lhs = each(lambda x, y: jnp.concatenate([x, y.astype(BF16)], axis=0), at, rt)
        abk = each(lambda l, b_, k_: _dot_nt(l, jnp.concatenate([_bd(b_), _bd(k_)], axis=0)), lhs, bt, kt)
        tick()
        zero = jnp.zeros((L, LANES), F32)
        n_ab = each(lambda x: jnp.where(strict, x[:L, :LANES], zero).astype(BF16), abk)
        n_ak = each(lambda x: jnp.where(strict, x[:L, LANES:], zero).astype(BF16), abk)
        a_rb = each(lambda x: jnp.where(incl, x[L:, :LANES], zero).astype(BF16), abk)
        a_rk = each(lambda x: jnp.where(incl, x[L:, LANES:], zero).astype(BF16), abk)

        t = each(lambda x: jnp.where(eye, 1.0, 0.0) + x.astype(F32), n_ab)
        pw = each(lambda x: _mm(x, x).astype(BF16), n_ab)
        tick()
        steps = CHUNK.bit_length() - 2
        for s in range(steps - 1):
            both = each(lambda p_, t_: _dot(p_, side(_bd(p_), _bd(t_.astype(BF16)))), pw, t)
            pw = each(lambda x: x[:, :LANES].astype(BF16), both)
            t = each(lambda t_, x: t_ + x[:, LANES:], t, both)
            tick()
        t = each(lambda t_, p_: (t_ + _mm(p_, t_)).astype(BF16), t, pw)
        tick()

        akv = each(lambda m, v_: _mm(m, v_).astype(BF16), n_ak, v)
        tick()
        pq = each(lambda t_, a_, kv: _dot(t_, side(_bd(a_), _bd(kv))), t, at, akv)
        p = each(lambda x: x[:, :LANES].astype(BF16), pq)
        q = each(lambda x: x[:, LANES:].astype(BF16), pq)
        tick()
        zeros_bd = jnp.zeros((LANES, LANES), BF16)
        rpy = each(lambda m1, m2, p_, q_, v_: _dot(
            side(m1, m2), jnp.concatenate([side(_bd(p_), _bd(q_)), side(zeros_bd, _bd(v_))], axis=0)),
            a_rb, a_rk, p, q, v)
        tick()
        zeros_l = jnp.zeros((L, LANES), BF16)
        gh = each(lambda b_, k_, p_, q_, v_: _dot_tn(
            jnp.concatenate([b_, k_], axis=0),
            jnp.concatenate([side(p_, q_), side(zeros_l, v_)], axis=0)), bh, kh, p, q, v)
        while hooks:
            tick()
        for n, (c, pr) in enumerate(chains):
            rows = pl.ds(aligned((c + LAG) * L, L), L)
            rp_ref[pr, rows, :] = rt[n] + rpy[n][:, :LANES]
            y0_ref[pr, rows, :] = rpy[n][:, LANES:]
            hadd_ref[pr, rows, :] = _fold(gh[n][:, LANES:])
            g_hi, g_lo = _split2(jnp.where(eye, g_l[n], 0.0) + _fold(gh[n][:, :LANES]))
            ghi_ref[pr, rows, :] = g_hi
            glo_ref[pr, rows, :] = g_lo

    def advance(slot, states):
        rows = pl.ds(aligned(slot * L, L), L)
        split = [_split2(_bd(h)) for h in states]
        new = []
        for pr in range(pairs):
            h_hi, h_lo = split[pr]
            g_hi = ghi_ref[pr, rows, :]
            new.append(_dot(g_hi, h_hi) + _dot(glo_ref[pr, rows, :], h_hi) + _dot(g_hi, h_lo)
                       + hadd_ref[pr, rows, :])
        for pr in range(pairs):
            y0_ref[pr, rows, :] = _dot(rp_ref[pr, rows, :].astype(BF16), split[pr][0]) + y0_ref[pr, rows, :]
        return tuple(new)

    def finish(tiles, hooks):
        hooks = list(hooks)

        def tick():
            if hooks:
                hooks.pop(0)()

        rows = [pl.ds(t * NORM_ROWS, NORM_ROWS) for t, _ in tiles]
        y = [y0_ref[pr, pl.ds(t * NORM_ROWS + LAG * L, NORM_ROWS), :] for t, pr in tiles]
        mean = [_head_sum(x, ones_bd) * (1.0 / HEAD_DIM) for x in y]
        tick()
        yc = [x - m for x, m in zip(y, mean)]
        var = [_head_sum(x * x, ones_bd) * (1.0 / HEAD_DIM) for x in yc]
        tick()
        for (_, pr), rw, x, s in zip(tiles, rows, yc, var):
            prm = prm_ref[pr]
            yn = x * lax.rsqrt(s + GN_EPS) * prm[8:9] + prm[9:10] + bonus_ref[pr, rw, :]
            z = z_ref[pr, rw, :]
            o_ref[rw, pr * LANES:(pr + 1) * LANES] = (yn * (z / (1.0 + jnp.exp(-z)))).astype(o_ref.dtype)
        while hooks:
            tick()

    for pr in range(pairs):
        ghi_ref[pr, 0:LAG * L, :] = jnp.zeros((LAG * L, LANES), BF16)
        glo_ref[pr, 0:LAG * L, :] = jnp.zeros((LAG * L, LANES), BF16)
        for ref in (hadd_ref, rp_ref, y0_ref):
            ref[pr, 0:LAG * L, :] = jnp.zeros((LAG * L, LANES), F32)

    for _ in stage_a(0, 0):
        pass

    def body(i, states):
        states = list(states)
        parity = i & 1
        gen = stage_a(jnp.minimum(i + 1, n_groups - 1), 1 - parity)

        def step(u):
            def hook():
                states[:] = advance(i * LAG + u, tuple(states))
            return hook

        def next_a():
            next(gen, None)

        hooks = [[] for _ in range(B_TICKS)]
        for n in A_TICKS:
            hooks[n].append(next_a)
        for u in range(LAG):
            hooks[u * B_TICKS // LAG].append(step(u))
        stage_b(i, parity, hooks)
        for _ in gen:
            pass
        return tuple(states)

    states = lax.fori_loop(0, n_groups, body,
                           tuple(jnp.zeros((HEAD_DIM, LANES), F32) for _ in range(pairs)))

    states = list(states)

    def tail_step(u):
        def hook():
            states[:] = advance(n_chunks + u, tuple(states))
        return hook

    tail = [tail_step(u) for u in range(LAG)]
    n_tiles = seq // NORM_ROWS
    groups = [list(range(g, g + NORM_UNROLL)) for g in range(0, n_tiles, NORM_UNROLL)]
    n_early = sum(1 for g in groups if (g[-1] + 1) * NORM_ROWS <= seq - LAG * L)
    per_group = -(-LAG // max(n_early, 1))
    for g in groups:
        if (g[-1] + 1) * NORM_ROWS <= seq - LAG * L:
            mine, tail = tail[:per_group], tail[per_group:]
        else:
            while tail:
                tail.pop(0)()
            mine = []
        finish([(t, pr) for t in g for pr in range(pairs)], mine)


def _rwkv(p3, plora, prm, wup, *, batch, seq):
    pairs = RWKV_PAIRS_PER_STEP
    groups = RWKV_WIDTH // LANES // pairs
    n_chunks = seq // CHUNK
    assert n_chunks % LAG == 0 and n_chunks >= 2 * LAG
    assert seq % (NORM_ROWS * NORM_UNROLL) == 0

    def slab(base):
        return pl.BlockSpec((pairs, seq, LANES), lambda b, h: (base // pairs + h, b, 0))

    def per_pair(rows, dtype):
        return pltpu.VMEM((pairs, rows, LANES), dtype)

    return pl.pallas_call(
        functools.partial(_rwkv_kernel, seq=seq, pairs=pairs),
        grid=(batch, groups),
        in_specs=[
            slab(SLAB_R), slab(SLAB_K), slab(SLAB_V), slab(SLAB_ZA),
            pl.BlockSpec((seq, LANES), lambda b, h: (b, 0)),
            pl.BlockSpec((pairs, 16, LANES), lambda b, h: (h, 0, 0)),
            pl.BlockSpec((pairs, LANES, 2 * LANES), lambda b, h: (h, 0, 0)),
        ],
        out_specs=pl.BlockSpec((seq, pairs * LANES), lambda b, h: (b, h)),
        out_shape=jax.ShapeDtypeStruct((batch * seq, RWKV_WIDTH), BF16),
        scratch_shapes=[
            per_pair(seq + LAG * CHUNK, BF16), per_pair(seq + LAG * CHUNK, BF16),
            per_pair(seq + LAG * CHUNK, F32), per_pair(seq + LAG * CHUNK, F32),
            per_pair(seq + LAG * CHUNK, F32), per_pair(seq, F32),
            pltpu.VMEM((2, LAG * pairs, 6, CHUNK, LANES), BF16),
            pltpu.VMEM((2, LAG * pairs, CHUNK + 8, LANES), F32),
        ],
        compiler_params=pltpu.CompilerParams(
            dimension_semantics=("arbitrary", "arbitrary"), vmem_limit_bytes=VMEM_LIMIT),
        name="rwkv7_mix",
    )(p3, p3, p3, p3, plora, prm, wup)


def _moba_kernel(q_ref, k_ref, v_ref, z_ref, qw_ref, kw_ref, o_ref,
                 q0_ref, q1_ref, ks_ref, vt_ref, gate_ref, *, seq):
    blk = MOBA_BLOCK
    nb = seq // blk
    ones_bd = _head_ones()
    first = _lane_iota((seq, LANES)) < HEAD_DIM

    def normed(ref, w_ref):
        x = ref[...]
        ms = _head_sum(x * x, ones_bd) * (1.0 / HEAD_DIM)
        return x * lax.rsqrt(ms + RMS_EPS) * w_ref[...]

    qn = normed(q_ref, qw_ref)
    kn = normed(k_ref, kw_ref)
    qs = qn * (HEAD_DIM ** -0.5 * LOG2_E)
    q0_ref[...] = jnp.where(first, qs, 0.0).astype(BF16)
    q1_ref[...] = jnp.where(first, 0.0, qs).astype(BF16)
    ks_ref[...] = kn.astype(BF16)
    eye = jnp.where(_lane_iota((LANES, LANES)) == _row_iota((LANES, LANES)), 1.0, 0.0).astype(BF16)
    v_t = _dot_nt(eye, v_ref[...].astype(BF16)).astype(BF16)
    for h in range(2):
        vt_ref[h, 0:HEAD_DIM, :] = v_t[h * HEAD_DIM:(h + 1) * HEAD_DIM, :]
        vt_ref[h, HEAD_DIM:, :] = jnp.ones((ONES_ROWS, seq), BF16)

    km = jnp.mean(kn.reshape(nb, blk, LANES), axis=1)
    first_nb = _lane_iota((nb, LANES)) < HEAD_DIM
    km_bd = jnp.concatenate([jnp.where(first_nb, km, 0.0), jnp.where(first_nb, 0.0, km)], axis=0)
    km_hi, km_lo = _split2(km_bd)
    q_hi, q_lo = _split2(qn)
    gate_ref[...] = _dot_nt(km_hi, q_hi) + _dot_nt(km_hi, q_lo) + _dot_nt(km_lo, q_hi)

    causal_t = _row_iota((blk, blk)) <= _lane_iota((blk, blk))
    blk_row = _row_iota((nb, blk))

    def scores(i, h):
        qh_ref = q0_ref if h == 0 else q1_ref
        return _dot_nt(ks_ref[0:(i + 1) * blk, :], qh_ref[i * blk:(i + 1) * blk, :])

    def attend(i, h, s):
        pieces = []
        if i > MOBA_TOPK:
            gate = gate_ref[h * nb:(h + 1) * nb, i * blk:(i + 1) * blk]
        for j in range(i):
            sj = s[j * blk:(j + 1) * blk, :]
            if i > MOBA_TOPK:
                gj = gate[j:j + 1, :]
                beats = (blk_row < i) & ((gate > gj) | ((gate == gj) & (blk_row < j)))
                rank = jnp.sum(jnp.where(beats, 1.0, 0.0), axis=0, keepdims=True)
                sj = jnp.where(rank < MOBA_TOPK, sj, NEG_INF)
            pieces.append(sj)
        pieces.append(jnp.where(causal_t, s[i * blk:, :], NEG_INF))
        s = jnp.concatenate(pieces, axis=0) if i > 0 else pieces[0]
        m = jnp.max(s, axis=0, keepdims=True)
        e = jnp.exp2(s - m)
        pv = _dot(vt_ref[h, :, 0:(i + 1) * blk], e.astype(BF16))
        return pv[:HEAD_DIM] * (1.0 / pv[HEAD_DIM:HEAD_DIM + 1])

    units = [(i, h) for i in range(nb) for h in range(2)]
    ahead = [scores(*u) for u in units[:SCORE_LOOKAHEAD]]
    outs = []
    for n, (i, h) in enumerate(units):
        s_cur = ahead.pop(0)
        if n + SCORE_LOOKAHEAD < len(units):
            ahead.append(scores(*units[n + SCORE_LOOKAHEAD]))
        outs.append(attend(i, h, s_cur))
        if h == 1:
            rows = slice(i * blk, (i + 1) * blk)
            o = jnp.concatenate([outs[-2], outs[-1]], axis=0).T
            z = z_ref[rows, :]
            o_ref[rows, :] = (o * (z / (1.0 + jnp.exp(-z)))).astype(o_ref.dtype)


def _moba(p3, qw, kw, *, batch, seq):
    hp = MOBA_WIDTH // LANES

    def slab(base):
        return pl.BlockSpec((None, seq, LANES), lambda b, h: (base + h, b, 0))

    return pl.pallas_call(
        functools.partial(_moba_kernel, seq=seq),
        grid=(batch, hp),
        in_specs=[
            slab(SLAB_Q), slab(SLAB_KQ), slab(SLAB_VQ), slab(SLAB_ZB),
            pl.BlockSpec((1, LANES), lambda b, h: (0, 0)),
            pl.BlockSpec((1, LANES), lambda b, h: (0, 0)),
        ],
        out_specs=pl.BlockSpec((seq, LANES), lambda b, h: (b, h)),
        out_shape=jax.ShapeDtypeStruct((batch * seq, MOBA_WIDTH), BF16),
        scratch_shapes=[
            pltpu.VMEM((seq, LANES), BF16), pltpu.VMEM((seq, LANES), BF16), pltpu.VMEM((seq, LANES), BF16),
            pltpu.VMEM((2, HEAD_DIM + ONES_ROWS, seq), BF16),
            pltpu.VMEM((2 * (seq // MOBA_BLOCK), seq), F32),
        ],
        compiler_params=pltpu.CompilerParams(
            dimension_semantics=("arbitrary", "arbitrary"), vmem_limit_bytes=VMEM_LIMIT),
        name="moba_attention",
    )(p3, p3, p3, p3, qw, kw)


def _out_kernel(x_ref, ya_ref, yb_ref, g_ref, wa_ref, wb_ref, wo_ref, o_ref, *, d_model):
    n = d_model // LANES
    pa = _dot(ya_ref[...], wa_ref[...])
    pb = _dot(yb_ref[...], wb_ref[...])
    ga = jnp.concatenate([g_ref[c] for c in range(n)], axis=1)
    gb = jnp.concatenate([g_ref[n + c] for c in range(n)], axis=1)
    merged = pa / (1.0 + jnp.exp(-ga)) + pb / (1.0 + jnp.exp(-gb))
    o_ref[...] = x_ref[...] + _dot(merged.astype(BF16), wo_ref[...])


def _out_proj(x2, ya, yb, p3, wa, wb, wo, *, tm):
    m, d = x2.shape
    n_g = 2 * d // LANES
    const = dict(pipeline_mode=pl.Buffered(1))
    return pl.pallas_call(
        functools.partial(_out_kernel, d_model=d),
        grid=(m // tm,),
        in_specs=[
            pl.BlockSpec((tm, d), lambda i: (i, 0)),
            pl.BlockSpec((tm, RWKV_WIDTH), lambda i: (i, 0)),
            pl.BlockSpec((tm, MOBA_WIDTH), lambda i: (i, 0)),
            pl.BlockSpec((n_g, tm, LANES), lambda i: (SLAB_G // n_g, i, 0)),
            pl.BlockSpec((RWKV_WIDTH, d), lambda i: (0, 0), **const),
            pl.BlockSpec((MOBA_WIDTH, d), lambda i: (0, 0), **const),
            pl.BlockSpec((d, d), lambda i: (0, 0), **const),
        ],
        out_specs=pl.BlockSpec((tm, d), lambda i: (i, 0)),
        out_shape=jax.ShapeDtypeStruct((m, d), F32),
        compiler_params=pltpu.CompilerParams(
            dimension_semantics=("arbitrary",), vmem_limit_bytes=VMEM_LIMIT),
        name="merge_out_proj",
    )(x2, ya, yb, p3, wa, wb, wo)


def _layer(x2, batch, seq, norm_w, w_in, mu_r, mu_k, mu_v, mu_w, mu_a, w0, w_decay_up, a0, w_iclr_up,
           k_k, k_a, r_k, gn_w, gn_b, q_norm_w, k_norm_w, w_proj_rwkv, w_proj_moba, w_out):
    d = x2.shape[1]
    assert d == (N_SLABS * LANES - SLAB_G * LANES) // 2
    p3, plora = _in_proj(x2, norm_w.reshape(1, d), w_in, tm=min(2048, x2.shape[0]), tn=512)

    hp = RWKV_WIDTH // LANES
    vecs = jnp.stack([mu_r, mu_k, mu_v, w0, a0, k_k, k_a, r_k, gn_w, gn_b]).reshape(10, hp, LANES)
    mu_l = jnp.broadcast_to(jnp.concatenate([mu_w, mu_a]).reshape(1, 1, LANES), (1, hp, LANES))
    prm = jnp.concatenate([vecs, mu_l, jnp.zeros((5, hp, LANES), F32)], axis=0).transpose(1, 0, 2)
    zeros = jnp.zeros((hp, LORA, LANES), F32)
    wd = w_decay_up.reshape(LORA, hp, LANES).transpose(1, 0, 2)
    wa_up = w_iclr_up.reshape(LORA, hp, LANES).transpose(1, 0, 2)
    wup = jnp.concatenate([jnp.concatenate([wd, zeros], axis=2),
                           jnp.concatenate([zeros, wa_up], axis=2)], axis=1)

    ya = _rwkv(p3, plora, prm, wup, batch=batch, seq=seq)
    qw = jnp.tile(q_norm_w, 2).reshape(1, LANES)
    kw = jnp.tile(k_norm_w, 2).reshape(1, LANES)
    yb = _moba(p3, qw, kw, batch=batch, seq=seq)
    return _out_proj(x2, ya, yb, p3, w_proj_rwkv.astype(BF16), w_proj_moba.astype(BF16),
                     w_out.astype(BF16), tm=256)


def kernel(x, norm_w, w_in, mu_r, mu_k, mu_v, mu_w, mu_a, w0, w_decay_up, a0, w_iclr_up, k_k, k_a, r_k,
           gn_w, gn_b, q_norm_w, k_norm_w, w_proj_rwkv, w_proj_moba, w_out):
    batch, seq, d = x.shape
    assert seq % MOBA_BLOCK == 0 and seq % CHUNK == 0
    params = (norm_w, w_in, mu_r, mu_k, mu_v, mu_w, mu_a, w0, w_decay_up, a0, w_iclr_up, k_k, k_a, r_k,
              gn_w, gn_b, q_norm_w, k_norm_w, w_proj_rwkv, w_proj_moba, w_out)
    x2 = x.reshape(batch * seq, d)
    for layer in range(norm_w.shape[0]):
        x2 = _layer(x2, batch, seq, *[p[layer] for p in params])
    return x2.reshape(batch, seq, d)
```

```python
import functools

import jax
import jax.numpy as jnp
from jax import lax
from jax.experimental import pallas as pl
from jax.experimental.pallas import tpu as pltpu

F32 = jnp.float32
BF16 = jnp.bfloat16

LANES = 128
HEAD_DIM = 64
RWKV_WIDTH = 1024
MOBA_WIDTH = 1024
LORA = 64
MOBA_BLOCK = 256
MOBA_TOPK = 3
RMS_EPS = 1e-6
GN_EPS = 64e-5
NEG_INF = -1e30
LOG2_E = 1.4426950408889634
CHUNK = 64
VMEM_LIMIT = 56 * 1024 * 1024
RWKV_PAIRS_PER_STEP = 2
LAG = 8
B_TICKS = 10
A_TICKS = (0, 2, 4, 6)
NORM_ROWS = 256
NORM_UNROLL = 2
SCORE_LOOKAHEAD = 3
ONES_ROWS = 16

SLAB_R, SLAB_K, SLAB_V, SLAB_ZA = 0, 8, 16, 24
SLAB_Q, SLAB_KQ, SLAB_VQ, SLAB_ZB = 32, 40, 48, 56
SLAB_G = 64
N_SLABS = 96


def _split2(x):
    hi = x.astype(BF16)
    lo = (x - hi.astype(F32)).astype(BF16)
    return hi, lo


def _split_lanes(x):
    hi, lo = _split2(x)
    return jnp.concatenate([hi, lo], axis=1)


def _dot(a, b):
    return jnp.dot(a, b, preferred_element_type=F32)


def _dot_nt(a, b):
    return lax.dot_general(a, b, (((1,), (1,)), ((), ())), preferred_element_type=F32)


def _dot_tn(a, b):
    return lax.dot_general(a, b, (((0,), (0,)), ((), ())), preferred_element_type=F32)


def _lane_iota(shape):
    return lax.broadcasted_iota(jnp.int32, shape, len(shape) - 1)


def _row_iota(shape):
    return lax.broadcasted_iota(jnp.int32, shape, 0)


def _head_ones():
    r = _row_iota((2 * LANES, LANES)) & (LANES - 1)
    c = _lane_iota((2 * LANES, LANES))
    return jnp.where((r < HEAD_DIM) == (c < HEAD_DIM), 1.0, 0.0).astype(BF16)


def _head_sum(x, ones2):
    return _dot(_split_lanes(x), ones2)


def _bd(x):
    first = _lane_iota(x.shape) < HEAD_DIM
    zero = jnp.zeros_like(x)
    return jnp.concatenate([jnp.where(first, x, zero), jnp.where(first, zero, x)], axis=0)


def _fold(m):
    first = _lane_iota((HEAD_DIM, LANES)) < HEAD_DIM
    return jnp.where(first, m[:HEAD_DIM], m[HEAD_DIM:])


def _in_proj_kernel(x_hbm, nw_ref, w_ref, wl_ref, p_ref, pl_ref, x_buf, h_ref, x_sem, *, slabs, tm):
    i, j = pl.program_id(0), pl.program_id(1)

    def x_copy(tile):
        return pltpu.make_async_copy(x_hbm.at[pl.ds(tile * tm, tm), :], x_buf, x_sem)

    @pl.when(j == 0)
    def _():
        @pl.when(i == 0)
        def _():
            x_copy(0).start()

        x_copy(i).wait()
        x = x_buf[...]
        ms = jnp.mean(x * x, axis=-1, keepdims=True)
        h = ((x * lax.rsqrt(ms + RMS_EPS)) * nw_ref[...]).astype(BF16)
        h_ref[...] = h
        pl_ref[...] = _dot(h, wl_ref[...].astype(BF16))

    @pl.when((j == 1) & (i + 1 < pl.num_programs(0)))
    def _():
        x_copy(i + 1).start()

    acc = _dot(h_ref[...], w_ref[...].astype(BF16))
    for c in range(slabs):
        p_ref[c] = acc[:, c * LANES:(c + 1) * LANES]


def _in_proj(x2, norm_w, w_in, *, tm, tn):
    m, d = x2.shape
    lora_at = 4 * RWKV_WIDTH
    n = w_in.shape[1] - 2 * LORA
    slabs = tn // LANES
    assert n == N_SLABS * LANES and lora_at % tn == 0 and n % tn == 0 and 2 * LORA == LANES
    assert n // tn >= 2 and m % tm == 0
    whole = pl.Element(d)
    return pl.pallas_call(
        functools.partial(_in_proj_kernel, slabs=slabs, tm=tm),
        grid=(m // tm, n // tn),
        in_specs=[
            pl.BlockSpec(memory_space=pl.ANY),
            pl.BlockSpec((1, d), lambda i, j: (0, 0)),
            pl.BlockSpec((whole, pl.Element(tn)),
                         lambda i, j: (0, (j * slabs + jnp.where(j * tn >= lora_at, 1, 0)) * LANES)),
            pl.BlockSpec((whole, pl.Element(LANES)), lambda i, j: (0, lora_at)),
        ],
        out_specs=[
            pl.BlockSpec((slabs, tm, LANES), lambda i, j: (j, i, 0)),
            pl.BlockSpec((tm, LANES), lambda i, j: (i, 0)),
        ],
        out_shape=[
            jax.ShapeDtypeStruct((n // LANES, m, LANES), F32),
            jax.ShapeDtypeStruct((m, LANES), F32),
        ],
        scratch_shapes=[pltpu.VMEM((tm, d), F32), pltpu.VMEM((tm, d), BF16), pltpu.SemaphoreType.DMA],
        compiler_params=pltpu.CompilerParams(
            dimension_semantics=("arbitrary", "arbitrary"), vmem_limit_bytes=VMEM_LIMIT),
        name="in_proj",
    )(x2, norm_w, w_in, w_in)


def _mm(a, b):
    return _dot(a.astype(BF16), _bd(b).astype(BF16))


def _rwkv_kernel(r_ref, k_ref, v_ref, z_ref, lo_ref, prm_ref, wup_ref, o_ref,
                 ghi_ref, glo_ref, hadd_ref, rp_ref, y0_ref, bonus_ref, sbf_ref, sf_ref,
                 *, seq, pairs):
    n_chunks = seq // CHUNK
    n_groups = n_chunks // LAG
    L = CHUNK
    n_chains = LAG * pairs
    ones_bd = _head_ones()
    row = _row_iota((L, LANES))
    lane = _lane_iota((L, LANES))
    col = lane & (HEAD_DIM - 1)
    strict = col < row
    incl = col <= row
    eye = col == row
    first_lanes = lane < HEAD_DIM
    tri = jnp.where(_lane_iota((L, L)) <= _row_iota((L, L)), 1.0, 0.0).astype(BF16)
    tri2 = jnp.concatenate([tri, tri], axis=1)

    def aligned(x, m):
        return x if isinstance(x, int) else pl.multiple_of(x, m)

    def shifted(ref, c, mu):
        cur = ref[pl.ds(aligned(c * L, L), L), :]
        if isinstance(c, int) and c == 0:
            prev = jnp.where(row == 0, 0.0, pltpu.roll(cur, 1, 0))
        else:
            prev = ref[pl.ds(c * L - 1, L), :]
        return cur + (prev - cur) * mu

    def each(fn, *lists):
        return [fn(*args) for args in zip(*lists)]

    def chains_of(group):
        return [(group * LAG + u, pr) for u in range(LAG) for pr in range(pairs)]

    def stage_a(group, parity):
        chains = chains_of(group)
        prms = [prm_ref[pr] for _, pr in chains]
        rows = [pl.ds(aligned(c * L, L), L) for c, _ in chains]
        r = [shifted(r_ref.at[pr], c, prm[0:1]) for (c, pr), prm in zip(chains, prms)]
        k = [shifted(k_ref.at[pr], c, prm[1:2]) for (c, pr), prm in zip(chains, prms)]
        v = [shifted(v_ref.at[pr], c, prm[2:3]) for (c, pr), prm in zip(chains, prms)]
        xl = [shifted(lo_ref, c, prm[10:11]) for (c, pr), prm in zip(chains, prms)]

        feat = each(lambda x: _split_lanes(jnp.where(first_lanes, jnp.tanh(x), x)), xl)
        wups = [_split2(wup_ref[pr]) for _, pr in chains]
        up = each(lambda f, w: _dot(f, jnp.concatenate([w[0], w[0]], axis=0)) + _dot(f[:, :LANES], w[1]),
                  feat, wups)
        yield

        def decay_log(u, prm):
            neg = -(prm[3:4] + u[:, :LANES])
            softplus = jnp.maximum(neg, 0.0) + jnp.log(1.0 + jnp.exp(-jnp.abs(neg)))
            return -jnp.exp(-softplus - 0.5)

        logw = each(decay_log, up, prms)
        a = each(lambda u, prm: 1.0 / (1.0 + jnp.exp(-(prm[4:5] + u[:, LANES:]))), up, prms)
        kk = each(lambda k_, prm: k_ * prm[5:6], k, prms)
        ksq = each(lambda x: _head_sum(x * x, ones_bd), kk)
        cum = each(lambda x: _dot(tri2, jnp.concatenate(_split2(x), axis=0)), logw)
        yield
        kk = each(lambda x, s: x / jnp.maximum(jnp.sqrt(s), 1e-12), kk, ksq)
        k2 = each(lambda k_, a_, prm: k_ * (1.0 + (a_ - 1.0) * prm[6:7]), k, a, prms)
        b = each(lambda x, a_: x * a_, kk, a)
        rk = each(lambda r_, k_, prm: _head_sum(r_ * k_ * prm[7:8], ones_bd), r, k2, prms)
        yield
        for (c, pr), rw, x, v_ in zip(chains, rows, rk, v):
            bonus_ref[pr, rw, :] = x * v_

        cum_l = each(lambda x: x[L - 1:L, :], cum)
        g_inv = each(lambda x: jnp.exp(-x), cum)
        g_rest = each(lambda x, xl_: jnp.exp(xl_ - x), cum, cum_l)
        staged = [
            each(lambda kk_, x, lw: -kk_ * jnp.exp(x - lw), kk, cum, logw),
            each(lambda b_, g: b_ * g, b, g_inv),
            each(lambda k_, g: k_ * g, k2, g_inv),
            each(lambda b_, g: b_ * g, b, g_rest),
            each(lambda k_, g: k_ * g, k2, g_rest),
            v,
        ]
        rt = each(lambda r_, x: r_ * jnp.exp(x), r, cum)
        for n in range(n_chains):
            for s, arrs in enumerate(staged):
                sbf_ref[parity, n, s] = arrs[n].astype(BF16)
            sf_ref[parity, n, 0:L, :] = rt[n]
            sf_ref[parity, n, L:L + 8, :] = jnp.broadcast_to(jnp.exp(cum_l[n]), (8, LANES))

    def stage_b(group, parity, hooks):
        hooks = list(hooks)

        def tick():
            if hooks:
                for hook in hooks.pop(0):
                    hook()

        chains = chains_of(group)
        idx = range(n_chains)
        at, bt, kt, bh, kh, v = ([sbf_ref[parity, n, s] for n in idx] for s in range(6))
        rt = [sf_ref[parity, n, 0:L, :] for n in idx]
        g_l = [sf_ref[parity, n, L:L + 1, :] for n in idx]

        def side(x, y):
            return jnp.concatenate([x, y], axis=1)

        lhs = each(lambda x, y: jnp.concatenate([x, y.astype(BF16)], axis=0), at, rt)
        abk = each(lambda l, b_, k_: _dot_nt(l, jnp.concatenate([_bd(b_), _bd(k_)], axis=0)), lhs, bt, kt)
        tick()
        zero = jnp.zeros((L, LANES), F32)
        n_ab = each(lambda x: jnp.where(strict, x[:L, :LANES], zero).astype(BF16), abk)
        n_ak = each(lambda x: jnp.where(strict, x[:L, LANES:], zero).astype(BF16), abk)
        a_rb = each(lambda x: jnp.where(incl, x[L:, :LANES], zero).astype(BF16), abk)
        a_rk = each(lambda x: jnp.where(incl, x[L:, LANES:], zero).astype(BF16), abk)

        t = each(lambda x: jnp.where(eye, 1.0, 0.0) + x.astype(F32), n_ab)
        pw = each(lambda x: _mm(x, x).astype(BF16), n_ab)
        tick()
        steps = CHUNK.bit_length() - 2
        for s in range(steps - 1):
            both = each(lambda p_, t_: _dot(p_, side(_bd(p_), _bd(t_.astype(BF16)))), pw, t)
            pw = each(lambda x: x[:, :LANES].astype(BF16), both)
            t = each(lambda t_, x: t_ + x[:, LANES:], t, both)
            tick()
        t = each(lambda t_, p_: (t_ + _mm(p_, t_)).astype(BF16), t, pw)
        tick()

        akv = each(lambda m, v_: _mm(m, v_).astype(BF16), n_ak, v)
        tick()
        pq = each(lambda t_, a_, kv: _dot(t_, side(_bd(a_), _bd(kv))), t, at, akv)
        p = each(lambda x: x[:, :LANES].astype(BF16), pq)
        q = each(lambda x: x[:, LANES:].astype(BF16), pq)
        tick()
        zeros_bd = jnp.zeros((LANES, LANES), BF16)
        rpy = each(lambda m1, m2, p_, q_, v_: _dot(
            side(m1, m2), jnp.concatenate([side(_bd(p_), _bd(q_)), side(zeros_bd, _bd(v_))], axis=0)),
            a_rb, a_rk, p, q, v)
        tick()
        zeros_l = jnp.zeros((L, LANES), BF16)
        gh = each(lambda b_, k_, p_, q_, v_: _dot_tn(
            jnp.concatenate([b_, k_], axis=0),
            jnp.concatenate([side(p_, q_), side(zeros_l, v_)], axis=0)), bh, kh, p, q, v)
        while hooks:
            tick()
        for n, (c, pr) in enumerate(chains):
            rows = pl.ds(aligned((c + LAG) * L, L), L)
            rp_ref[pr, rows, :] = rt[n] + rpy[n][:, :LANES]
            y0_ref[pr, rows, :] = rpy[n][:, LANES:]
            hadd_ref[pr, rows, :] = _fold(gh[n][:, LANES:])
            g_hi, g_lo = _split2(jnp.where(eye, g_l[n], 0.0) + _fold(gh[n][:, :LANES]))
            ghi_ref[pr, rows, :] = g_hi
            glo_ref[pr, rows, :] = g_lo

    def advance(slot, states):
        rows = pl.ds(aligned(slot * L, L), L)
        split = [_split2(_bd(h)) for h in states]
        new = []
        for pr in range(pairs):
            h_hi, h_lo = split[pr]
            g_hi = ghi_ref[pr, rows, :]
            new.append(_dot(g_hi, h_hi) + _dot(glo_ref[pr, rows, :], h_hi) + _dot(g_hi, h_lo)
                       + hadd_ref[pr, rows, :])
        for pr in range(pairs):
            y0_ref[pr, rows, :] = _dot(rp_ref[pr, rows, :].astype(BF16), split[pr][0]) + y0_ref[pr, rows, :]
        return tuple(new)

    def finish(tiles, hooks):
        hooks = list(hooks)

        def tick():
            if hooks:
                hooks.pop(0)()

        rows = [pl.ds(t * NORM_ROWS, NORM_ROWS) for t, _ in tiles]
        y = [y0_ref[pr, pl.ds(t * NORM_ROWS + LAG * L, NORM_ROWS), :] for t, pr in tiles]
        mean = [_head_sum(x, ones_bd) * (1.0 / HEAD_DIM) for x in y]
        tick()
        yc = [x - m for x, m in zip(y, mean)]
        var = [_head_sum(x * x, ones_bd) * (1.0 / HEAD_DIM) for x in yc]
        tick()
        for (_, pr), rw, x, s in zip(tiles, rows, yc, var):
            prm = prm_ref[pr]
            yn = x * lax.rsqrt(s + GN_EPS) * prm[8:9] + prm[9:10] + bonus_ref[pr, rw, :]
            z = z_ref[pr, rw, :]
            o_ref[rw, pr * LANES:(pr + 1) * LANES] = (yn * (z / (1.0 + jnp.exp(-z)))).astype(o_ref.dtype)
        while hooks:
            tick()

    for pr in range(pairs):
        ghi_ref[pr, 0:LAG * L, :] = jnp.zeros((LAG * L, LANES), BF16)
        glo_ref[pr, 0:LAG * L, :] = jnp.zeros((LAG * L, LANES), BF16)
        for ref in (hadd_ref, rp_ref, y0_ref):
            ref[pr, 0:LAG * L, :] = jnp.zeros((LAG * L, LANES), F32)

    for _ in stage_a(0, 0):
        pass

    def body(i, states):
        states = list(states)
        parity = i & 1
        gen = stage_a(jnp.minimum(i + 1, n_groups - 1), 1 - parity)

        def step(u):
            def hook():
                states[:] = advance(i * LAG + u, tuple(states))
            return hook

        def next_a():
            next(gen, None)

        hooks = [[] for _ in range(B_TICKS)]
        for n in A_TICKS:
            hooks[n].append(next_a)
        for u in range(LAG):
            hooks[u * B_TICKS // LAG].append(step(u))
        stage_b(i, parity, hooks)
        for _ in gen:
            pass
        return tuple(states)

    states = lax.fori_loop(0, n_groups, body,
                           tuple(jnp.zeros((HEAD_DIM, LANES), F32) for _ in range(pairs)))

    states = list(states)

    def tail_step(u):
        def hook():
            states[:] = advance(n_chunks + u, tuple(states))
        return hook

    tail = [tail_step(u) for u in range(LAG)]
    n_tiles = seq // NORM_ROWS
    groups = [list(range(g, g + NORM_UNROLL)) for g in range(0, n_tiles, NORM_UNROLL)]
    n_early = sum(1 for g in groups if (g[-1] + 1) * NORM_ROWS <= seq - LAG * L)
    per_group = -(-LAG // max(n_early, 1))
    for g in groups:
        if (g[-1] + 1) * NORM_ROWS <= seq - LAG * L:
            mine, tail = tail[:per_group], tail[per_group:]
        else:
            while tail:
                tail.pop(0)()
            mine = []
        finish([(t, pr) for t in g for pr in range(pairs)], mine)


def _rwkv(p3, plora, prm, wup, *, batch, seq):
    pairs = RWKV_PAIRS_PER_STEP
    groups = RWKV_WIDTH // LANES // pairs
    n_chunks = seq // CHUNK
    assert n_chunks % LAG == 0 and n_chunks >= 2 * LAG
    assert seq % (NORM_ROWS * NORM_UNROLL) == 0

    def slab(base):
        return pl.BlockSpec((pairs, seq, LANES), lambda b, h: (base // pairs + h, b, 0))

    def per_pair(rows, dtype):
        return pltpu.VMEM((pairs, rows, LANES), dtype)

    return pl.pallas_call(
        functools.partial(_rwkv_kernel, seq=seq, pairs=pairs),
        grid=(batch, groups),
        in_specs=[
            slab(SLAB_R), slab(SLAB_K), slab(SLAB_V), slab(SLAB_ZA),
            pl.BlockSpec((seq, LANES), lambda b, h: (b, 0)),
            pl.BlockSpec((pairs, 16, LANES), lambda b, h: (h, 0, 0)),
            pl.BlockSpec((pairs, LANES, 2 * LANES), lambda b, h: (h, 0, 0)),
        ],
        out_specs=pl.BlockSpec((seq, pairs * LANES), lambda b, h: (b, h)),
        out_shape=jax.ShapeDtypeStruct((batch * seq, RWKV_WIDTH), BF16),
        scratch_shapes=[
            per_pair(seq + LAG * CHUNK, BF16), per_pair(seq + LAG * CHUNK, BF16),
            per_pair(seq + LAG * CHUNK, F32), per_pair(seq + LAG * CHUNK, F32),
            per_pair(seq + LAG * CHUNK, F32), per_pair(seq, F32),
            pltpu.VMEM((2, LAG * pairs, 6, CHUNK, LANES), BF16),
            pltpu.VMEM((2, LAG * pairs, CHUNK + 8, LANES), F32),
        ],
        compiler_params=pltpu.CompilerParams(
            dimension_semantics=("arbitrary", "arbitrary"), vmem_limit_bytes=VMEM_LIMIT),
        name="rwkv7_mix",
    )(p3, p3, p3, p3, plora, prm, wup)


def _moba_kernel(q_ref, k_ref, v_ref, z_ref, qw_ref, kw_ref, o_ref,
                 q0_ref, q1_ref, ks_ref, vt_ref, gate_ref, *, seq):
    blk = MOBA_BLOCK
    nb = seq // blk
    ones_bd = _head_ones()
    first = _lane_iota((seq, LANES)) < HEAD_DIM

    def normed(ref, w_ref):
        x = ref[...]
        ms = _head_sum(x * x, ones_bd) * (1.0 / HEAD_DIM)
        return x * lax.rsqrt(ms + RMS_EPS) * w_ref[...]

    qn = normed(q_ref, qw_ref)
    kn = normed(k_ref, kw_ref)
    qs = qn * (HEAD_DIM ** -0.5 * LOG2_E)
    q0_ref[...] = jnp.where(first, qs, 0.0).astype(BF16)
    q1_ref[...] = jnp.where(first, 0.0, qs).astype(BF16)
    ks_ref[...] = kn.astype(BF16)
    eye = jnp.where(_lane_iota((LANES, LANES)) == _row_iota((LANES, LANES)), 1.0, 0.0).astype(BF16)
    v_t = _dot_nt(eye, v_ref[...].astype(BF16)).astype(BF16)
    for h in range(2):
        vt_ref[h, 0:HEAD_DIM, :] = v_t[h * HEAD_DIM:(h + 1) * HEAD_DIM, :]
        vt_ref[h, HEAD_DIM:, :] = jnp.ones((ONES_ROWS, seq), BF16)

    km = jnp.mean(kn.reshape(nb, blk, LANES), axis=1)
    first_nb = _lane_iota((nb, LANES)) < HEAD_DIM
    km_bd = jnp.concatenate([jnp.where(first_nb, km, 0.0), jnp.where(first_nb, 0.0, km)], axis=0)
    km_hi, km_lo = _split2(km_bd)
    q_hi, q_lo = _split2(qn)
    gate_ref[...] = _dot_nt(km_hi, q_hi) + _dot_nt(km_hi, q_lo) + _dot_nt(km_lo, q_hi)

    causal_t = _row_iota((blk, blk)) <= _lane_iota((blk, blk))
    blk_row = _row_iota((nb, blk))

    def scores(i, h):
        qh_ref = q0_ref if h == 0 else q1_ref
        return _dot_nt(ks_ref[0:(i + 1) * blk, :], qh_ref[i * blk:(i + 1) * blk, :])

    def attend(i, h, s):
        pieces = []
        if i > MOBA_TOPK:
            gate = gate_ref[h * nb:(h + 1) * nb, i * blk:(i + 1) * blk]
        for j in range(i):
            sj = s[j * blk:(j + 1) * blk, :]
            if i > MOBA_TOPK:
                gj = gate[j:j + 1, :]
                beats = (blk_row < i) & ((gate > gj) | ((gate == gj) & (blk_row < j)))
                rank = jnp.sum(jnp.where(beats, 1.0, 0.0), axis=0, keepdims=True)
                sj = jnp.where(rank < MOBA_TOPK, sj, NEG_INF)
            pieces.append(sj)
        pieces.append(jnp.where(causal_t, s[i * blk:, :], NEG_INF))
        s = jnp.concatenate(pieces, axis=0) if i > 0 else pieces[0]
        m = jnp.max(s, axis=0, keepdims=True)
        e = jnp.exp2(s - m)
        pv = _dot(vt_ref[h, :, 0:(i + 1) * blk], e.astype(BF16))
        return pv[:HEAD_DIM] * (1.0 / pv[HEAD_DIM:HEAD_DIM + 1])

    units = [(i, h) for i in range(nb) for h in range(2)]
    ahead = [scores(*u) for u in units[:SCORE_LOOKAHEAD]]
    outs = []
    for n, (i, h) in enumerate(units):
        s_cur = ahead.pop(0)
        if n + SCORE_LOOKAHEAD < len(units):
            ahead.append(scores(*units[n + SCORE_LOOKAHEAD]))
        outs.append(attend(i, h, s_cur))
        if h == 1:
            rows = slice(i * blk, (i + 1) * blk)
            o = jnp.concatenate([outs[-2], outs[-1]], axis=0).T
            z = z_ref[rows, :]
            o_ref[rows, :] = (o * (z / (1.0 + jnp.exp(-z)))).astype(o_ref.dtype)


def _moba(p3, qw, kw, *, batch, seq):
    hp = MOBA_WIDTH // LANES

    def slab(base):
        return pl.BlockSpec((None, seq, LANES), lambda b, h: (base + h, b, 0))

    return pl.pallas_call(
        functools.partial(_moba_kernel, seq=seq),
        grid=(batch, hp),
        in_specs=[
            slab(SLAB_Q), slab(SLAB_KQ), slab(SLAB_VQ), slab(SLAB_ZB),
            pl.BlockSpec((1, LANES), lambda b, h: (0, 0)),
            pl.BlockSpec((1, LANES), lambda b, h: (0, 0)),
        ],
        out_specs=pl.BlockSpec((seq, LANES), lambda b, h: (b, h)),
        out_shape=jax.ShapeDtypeStruct((batch * seq, MOBA_WIDTH), BF16),
        scratch_shapes=[
            pltpu.VMEM((seq, LANES), BF16), pltpu.VMEM((seq, LANES), BF16), pltpu.VMEM((seq, LANES), BF16),
            pltpu.VMEM((2, HEAD_DIM + ONES_ROWS, seq), BF16),
            pltpu.VMEM((2 * (seq // MOBA_BLOCK), seq), F32),
        ],
        compiler_params=pltpu.CompilerParams(
            dimension_semantics=("arbitrary", "arbitrary"), vmem_limit_bytes=VMEM_LIMIT),
        name="moba_attention",
    )(p3, p3, p3, p3, qw, kw)


def _out_kernel(x_ref, ya_ref, yb_ref, g_ref, wa_ref, wb_ref, wo_ref, o_ref, *, d_model):
    n = d_model // LANES
    pa = _dot(ya_ref[...], wa_ref[...])
    pb = _dot(yb_ref[...], wb_ref[...])
    ga = jnp.concatenate([g_ref[c] for c in range(n)], axis=1)
    gb = jnp.concatenate([g_ref[n + c] for c in range(n)], axis=1)
    merged = pa / (1.0 + jnp.exp(-ga)) + pb / (1.0 + jnp.exp(-gb))
    o_ref[...] = x_ref[...] + _dot(merged.astype(BF16), wo_ref[...])


def _out_proj(x2, ya, yb, p3, wa, wb, wo, *, tm):
    m, d = x2.shape
    n_g = 2 * d // LANES
    const = dict(pipeline_mode=pl.Buffered(1))
    return pl.pallas_call(
        functools.partial(_out_kernel, d_model=d),
        grid=(m // tm,),
        in_specs=[
            pl.BlockSpec((tm, d), lambda i: (i, 0)),
            pl.BlockSpec((tm, RWKV_WIDTH), lambda i: (i, 0)),
            pl.BlockSpec((tm, MOBA_WIDTH), lambda i: (i, 0)),
            pl.BlockSpec((n_g, tm, LANES), lambda i: (SLAB_G // n_g, i, 0)),
            pl.BlockSpec((RWKV_WIDTH, d), lambda i: (0, 0), **const),
            pl.BlockSpec((MOBA_WIDTH, d), lambda i: (0, 0), **const),
            pl.BlockSpec((d, d), lambda i: (0, 0), **const),
        ],
        out_specs=pl.BlockSpec((tm, d), lambda i: (i, 0)),
        out_shape=jax.ShapeDtypeStruct((m, d), F32),
        compiler_params=pltpu.CompilerParams(
            dimension_semantics=("arbitrary",), vmem_limit_bytes=VMEM_LIMIT),
        name="merge_out_proj",
    )(x2, ya, yb, p3, wa, wb, wo)


def _layer(x2, batch, seq, norm_w, w_in, mu_r, mu_k, mu_v, mu_w, mu_a, w0, w_decay_up, a0, w_iclr_up,
           k_k, k_a, r_k, gn_w, gn_b, q_norm_w, k_norm_w, w_proj_rwkv, w_proj_moba, w_out):
    d = x2.shape[1]
    assert d == (N_SLABS * LANES - SLAB_G * LANES) // 2
    p3, plora = _in_proj(x2, norm_w.reshape(1, d), w_in, tm=min(2048, x2.shape[0]), tn=512)

    hp = RWKV_WIDTH // LANES
    vecs = jnp.stack([mu_r, mu_k, mu_v, w0, a0, k_k, k_a, r_k, gn_w, gn_b]).reshape(10, hp, LANES)
    mu_l = jnp.broadcast_to(jnp.concatenate([mu_w, mu_a]).reshape(1, 1, LANES), (1, hp, LANES))
    prm = jnp.concatenate([vecs, mu_l, jnp.zeros((5, hp, LANES), F32)], axis=0).transpose(1, 0, 2)
    zeros = jnp.zeros((hp, LORA, LANES), F32)
    wd = w_decay_up.reshape(LORA, hp, LANES).transpose(1, 0, 2)
    wa_up = w_iclr_up.reshape(LORA, hp, LANES).transpose(1, 0, 2)
    wup = jnp.concatenate([jnp.concatenate([wd, zeros], axis=2),
                           jnp.concatenate([zeros, wa_up], axis=2)], axis=1)

    ya = _rwkv(p3, plora, prm, wup, batch=batch, seq=seq)
    qw = jnp.tile(q_norm_w, 2).reshape(1, LANES)
    kw = jnp.tile(k_norm_w, 2).reshape(1, LANES)
    yb = _moba(p3, qw, kw, batch=batch, seq=seq)
    return _out_proj(x2, ya, yb, p3, w_proj_rwkv.astype(BF16), w_proj_moba.astype(BF16),
                     w_out.astype(BF16), tm=256)


def kernel(x, norm_w, w_in, mu_r, mu_k, mu_v, mu_w, mu_a, w0, w_decay_up, a0, w_iclr_up, k_k, k_a, r_k,
           gn_w, gn_b, q_norm_w, k_norm_w, w_proj_rwkv, w_proj_moba, w_out):
    batch, seq, d = x.shape
    assert seq % MOBA_BLOCK == 0 and seq % CHUNK == 0
    params = (norm_w, w_in, mu_r, mu_k, mu_v, mu_w, mu_a, w0, w_decay_up, a0, w_iclr_up, k_k, k_a, r_k,
              gn_w, gn_b, q_norm_w, k_norm_w, w_proj_rwkv, w_proj_moba, w_out)
    x2 = x.reshape(batch * seq, d)
    for layer in range(norm_w.shape[0]):
        x2 = _layer(x2, batch, seq, *[p[layer] for p in params])
    return x2.reshape(batch, seq, d)
```

```python
import functools

import jax
import jax.numpy as jnp
from jax import lax
from jax.experimental import pallas as pl
from jax.experimental.pallas import tpu as pltpu

F32 = jnp.float32
BF16 = jnp.bfloat16

LANES = 128
HEAD_DIM = 64
RWKV_WIDTH = 1024
MOBA_WIDTH = 1024
LORA = 64
MOBA_BLOCK = 256
MOBA_TOPK = 3
RMS_EPS = 1e-6
GN_EPS = 64e-5
NEG_INF = -1e30
LOG2_E = 1.4426950408889634
CHUNK = 64
VMEM_LIMIT = 56 * 1024 * 1024
RWKV_PAIRS_PER_STEP = 2
LAG = 8
B_TICKS = 10
A_TICKS = (0, 2, 4, 6)
NORM_ROWS = 256
NORM_UNROLL = 2
SCORE_LOOKAHEAD = 3
ONES_ROWS = 16

SLAB_R, SLAB_K, SLAB_V, SLAB_ZA = 0, 8, 16, 24
SLAB_Q, SLAB_KQ, SLAB_VQ, SLAB_ZB = 32, 40, 48, 56
SLAB_G = 64
N_SLABS = 96


def _split2(x):
    hi = x.astype(BF16)
    lo = (x - hi.astype(F32)).astype(BF16)
    return hi, lo


def _split_lanes(x):
    hi, lo = _split2(x)
    return jnp.concatenate([hi, lo], axis=1)


def _dot(a, b):
    return jnp.dot(a, b, preferred_element_type=F32)


def _dot_nt(a, b):
    return lax.dot_general(a, b, (((1,), (1,)), ((), ())), preferred_element_type=F32)


def _dot_tn(a, b):
    return lax.dot_general(a, b, (((0,), (0,)), ((), ())), preferred_element_type=F32)


def _lane_iota(shape):
    return lax.broadcasted_iota(jnp.int32, shape, len(shape) - 1)


def _row_iota(shape):
    return lax.broadcasted_iota(jnp.int32, shape, 0)


def _head_ones():
    r = _row_iota((2 * LANES, LANES)) & (LANES - 1)
    c = _lane_iota((2 * LANES, LANES))
    return jnp.where((r < HEAD_DIM) == (c < HEAD_DIM), 1.0, 0.0).astype(BF16)


def _head_sum(x, ones2):
    return _dot(_split_lanes(x), ones2)


def _bd(x):
    first = _lane_iota(x.shape) < HEAD_DIM
    zero = jnp.zeros_like(x)
    return jnp.concatenate([jnp.where(first, x, zero), jnp.where(first, zero, x)], axis=0)


def _fold(m):
    first = _lane_iota((HEAD_DIM, LANES)) < HEAD_DIM
    return jnp.where(first, m[:HEAD_DIM], m[HEAD_DIM:])


def _in_proj_kernel(x_hbm, nw_ref, w_ref, wl_ref, p_ref, pl_ref, x_buf, h_ref, x_sem, *, slabs, tm):
    i, j = pl.program_id(0), pl.program_id(1)

    def x_copy(tile):
        return pltpu.make_async_copy(x_hbm.at[pl.ds(tile * tm, tm), :], x_buf, x_sem)

    @pl.when(j == 0)
    def _():
        @pl.when(i == 0)
        def _():
            x_copy(0).start()

        x_copy(i).wait()
        x = x_buf[...]
        ms = jnp.mean(x * x, axis=-1, keepdims=True)
        h = ((x * lax.rsqrt(ms + RMS_EPS)) * nw_ref[...]).astype(BF16)
        h_ref[...] = h
        pl_ref[...] = _dot(h, wl_ref[...].astype(BF16))

    @pl.when((j == 1) & (i + 1 < pl.num_programs(0)))
    def _():
        x_copy(i + 1).start()

    acc = _dot(h_ref[...], w_ref[...].astype(BF16))
    for c in range(slabs):
        p_ref[c] = acc[:, c * LANES:(c + 1) * LANES]


def _in_proj(x2, norm_w, w_in, *, tm, tn):
    m, d = x2.shape
    lora_at = 4 * RWKV_WIDTH
    n = w_in.shape[1] - 2 * LORA
    slabs = tn // LANES
    assert n == N_SLABS * LANES and lora_at % tn == 0 and n % tn == 0 and 2 * LORA == LANES
    assert n // tn >= 2 and m % tm == 0
    whole = pl.Element(d)
    return pl.pallas_call(
        functools.partial(_in_proj_kernel, slabs=slabs, tm=tm),
        grid=(m // tm, n // tn),
        in_specs=[
            pl.BlockSpec(memory_space=pl.ANY),
            pl.BlockSpec((1, d), lambda i, j: (0, 0)),
            pl.BlockSpec((whole, pl.Element(tn)),
                         lambda i, j: (0, (j * slabs + jnp.where(j * tn >= lora_at, 1, 0)) * LANES)),
            pl.BlockSpec((whole, pl.Element(LANES)), lambda i, j: (0, lora_at)),
        ],
        out_specs=[
            pl.BlockSpec((slabs, tm, LANES), lambda i, j: (j, i, 0)),
            pl.BlockSpec((tm, LANES), lambda i, j: (i, 0)),
        ],
        out_shape=[
            jax.ShapeDtypeStruct((n // LANES, m, LANES), F32),
            jax.ShapeDtypeStruct((m, LANES), F32),
        ],
        scratch_shapes=[pltpu.VMEM((tm, d), F32), pltpu.VMEM((tm, d), BF16), pltpu.SemaphoreType.DMA],
        compiler_params=pltpu.CompilerParams(
            dimension_semantics=("arbitrary", "arbitrary"), vmem_limit_bytes=VMEM_LIMIT),
        name="in_proj",
    )(x2, norm_w, w_in, w_in)


def _mm(a, b):
    return _dot(a.astype(BF16), _bd(b).astype(BF16))


def _rwkv_kernel(r_ref, k_ref, v_ref, z_ref, lo_ref, prm_ref, wup_ref, o_ref,
                 ghi_ref, glo_ref, hadd_ref, rp_ref, y0_ref, bonus_ref, sbf_ref, sf_ref,
                 *, seq, pairs):
    n_chunks = seq // CHUNK
    n_groups = n_chunks // LAG
    L = CHUNK
    n_chains = LAG * pairs
    ones_bd = _head_ones()
    row = _row_iota((L, LANES))
    lane = _lane_iota((L, LANES))
    col = lane & (HEAD_DIM - 1)
    strict = col < row
    incl = col <= row
    eye = col == row
    first_lanes = lane < HEAD_DIM
    tri = jnp.where(_lane_iota((L, L)) <= _row_iota((L, L)), 1.0, 0.0).astype(BF16)
    tri2 = jnp.concatenate([tri, tri], axis=1)

    def aligned(x, m):
        return x if isinstance(x, int) else pl.multiple_of(x, m)

    def shifted(ref, c, mu):
        cur = ref[pl.ds(aligned(c * L, L), L), :]
        if isinstance(c, int) and c == 0:
            prev = jnp.where(row == 0, 0.0, pltpu.roll(cur, 1, 0))
        else:
            prev = ref[pl.ds(c * L - 1, L), :]
        return cur + (prev - cur) * mu

    def each(fn, *lists):
        return [fn(*args) for args in zip(*lists)]

    def chains_of(group):
        return [(group * LAG + u, pr) for u in range(LAG) for pr in range(pairs)]

    def stage_a(group, parity):
        chains = chains_of(group)
        prms = [prm_ref[pr] for _, pr in chains]
        rows = [pl.ds(aligned(c * L, L), L) for c, _ in chains]
        r = [shifted(r_ref.at[pr], c, prm[0:1]) for (c, pr), prm in zip(chains, prms)]
        k = [shifted(k_ref.at[pr], c, prm[1:2]) for (c, pr), prm in zip(chains, prms)]
        v = [shifted(v_ref.at[pr], c, prm[2:3]) for (c, pr), prm in zip(chains, prms)]
        xl = [shifted(lo_ref, c, prm[10:11]) for (c, pr), prm in zip(chains, prms)]

        feat = each(lambda x: _split_lanes(jnp.where(first_lanes, jnp.tanh(x), x)), xl)
        wups = [_split2(wup_ref[pr]) for _, pr in chains]
        up = each(lambda f, w: _dot(f, jnp.concatenate([w[0], w[0]], axis=0)) + _dot(f[:, :LANES], w[1]),
                  feat, wups)
        yield

        def decay_log(u, prm):
            neg = -(prm[3:4] + u[:, :LANES])
            softplus = jnp.maximum(neg, 0.0) + jnp.log(1.0 + jnp.exp(-jnp.abs(neg)))
            return -jnp.exp(-softplus - 0.5)

        logw = each(decay_log, up, prms)
        a = each(lambda u, prm: 1.0 / (1.0 + jnp.exp(-(prm[4:5] + u[:, LANES:]))), up, prms)
        kk = each(lambda k_, prm: k_ * prm[5:6], k, prms)
        ksq = each(lambda x: _head_sum(x * x, ones_bd), kk)
        cum = each(lambda x: _dot(tri2, jnp.concatenate(_split2(x), axis=0)), logw)
        yield
        kk = each(lambda x, s: x / jnp.maximum(jnp.sqrt(s), 1e-12), kk, ksq)
        k2 = each(lambda k_, a_, prm: k_ * (1.0 + (a_ - 1.0) * prm[6:7]), k, a, prms)
        b = each(lambda x, a_: x * a_, kk, a)
        rk = each(lambda r_, k_, prm: _head_sum(r_ * k_ * prm[7:8], ones_bd), r, k2, prms)
        yield
        for (c, pr), rw, x, v_ in zip(chains, rows, rk, v):
            bonus_ref[pr, rw, :] = x * v_

        cum_l = each(lambda x: x[L - 1:L, :], cum)
        g_inv = each(lambda x: jnp.exp(-x), cum)
        g_rest = each(lambda x, xl_: jnp.exp(xl_ - x), cum, cum_l)
        staged = [
            each(lambda kk_, x, lw: -kk_ * jnp.exp(x - lw), kk, cum, logw),
            each(lambda b_, g: b_ * g, b, g_inv),
            each(lambda k_, g: k_ * g, k2, g_inv),
            each(lambda b_, g: b_ * g, b, g_rest),
            each(lambda k_, g: k_ * g, k2, g_rest),
            v,
        ]
        rt = each(lambda r_, x: r_ * jnp.exp(x), r, cum)
        for n in range(n_chains):
            for s, arrs in enumerate(staged):
                sbf_ref[parity, n, s] = arrs[n].astype(BF16)
            sf_ref[parity, n, 0:L, :] = rt[n]
            sf_ref[parity, n, L:L + 8, :] = jnp.broadcast_to(jnp.exp(cum_l[n]), (8, LANES))

    def stage_b(group, parity, hooks):
        hooks = list(hooks)

        def tick():
            if hooks:
                for hook in hooks.pop(0):
                    hook()

        chains = chains_of(group)
        idx = range(n_chains)
        at, bt, kt, bh, kh, v = ([sbf_ref[parity, n, s] for n in idx] for s in range(6))
        rt = [sf_ref[parity, n, 0:L, :] for n in idx]
        g_l = [sf_ref[parity, n, L:L + 1, :] for n in idx]

        def side(x, y):
            return jnp.concatenate([x, y], axis=1)

        lhs = each(lambda x, y: jnp.concatenate([x, y.astype(BF16)], axis=0), at, rt)
        abk = each(lambda l, b_, k_: _dot_nt(l, jnp.concatenate([_bd(b_), _bd(k_)], axis=0)), lhs, bt, kt)
        tick()
        zero = jnp.zeros((L, LANES), F32)
        n_ab = each(lambda x: jnp.where(strict, x[:L, :LANES], zero).astype(BF16), abk)
        n_ak = each(lambda x: jnp.where(strict, x[:L, LANES:], zero).astype(BF16), abk)
        a_rb = each(lambda x: jnp.where(incl, x[L:, :LANES], zero).astype(BF16), abk)
        a_rk = each(lambda x: jnp.where(incl, x[L:, LANES:], zero).astype(BF16), abk)

        t = each(lambda x: jnp.where(eye, 1.0, 0.0) + x.astype(F32), n_ab)
        pw = each(lambda x: _mm(x, x).astype(BF16), n_ab)
        tick()
        steps = CHUNK.bit_length() - 2
        for s in range(steps - 1):
            both = each(lambda p_, t_: _dot(p_, side(_bd(p_), _bd(t_.astype(BF16)))), pw, t)
            pw = each(lambda x: x[:, :LANES].astype(BF16), both)
            t = each(lambda t_, x: t_ + x[:, LANES:], t, both)
            tick()
        t = each(lambda t_, p_: (t_ + _mm(p_, t_)).astype(BF16), t, pw)
        tick()

        akv = each(lambda m, v_: _mm(m, v_).astype(BF16), n_ak, v)
        tick()
        pq = each(lambda t_, a_, kv: _dot(t_, side(_bd(a_), _bd(kv))), t, at, akv)
        p = each(lambda x: x[:, :LANES].astype(BF16), pq)
        q = each(lambda x: x[:, LANES:].astype(BF16), pq)
        tick()
        zeros_bd = jnp.zeros((LANES, LANES), BF16)
        rpy = each(lambda m1, m2, p_, q_, v_: _dot(
            side(m1, m2), jnp.concatenate([side(_bd(p_), _bd(q_)), side(zeros_bd, _bd(v_))], axis=0)),
            a_rb, a_rk, p, q, v)
        tick()
        zeros_l = jnp.zeros((L, LANES), BF16)
        gh = each(lambda b_, k_, p_, q_, v_: _dot_tn(
            jnp.concatenate([b_, k_], axis=0),
            jnp.concatenate([side(p_, q_), side(zeros_l, v_)], axis=0)), bh, kh, p, q, v)
        while hooks:
            tick()
        for n, (c, pr) in enumerate(chains):
            rows = pl.ds(aligned((c + LAG) * L, L), L)
            rp_ref[pr, rows, :] = rt[n] + rpy[n][:, :LANES]
            y0_ref[pr, rows, :] = rpy[n][:, LANES:]
            hadd_ref[pr, rows, :] = _fold(gh[n][:, LANES:])
            g_hi, g_lo = _split2(jnp.where(eye, g_l[n], 0.0) + _fold(gh[n][:, :LANES]))
            ghi_ref[pr, rows, :] = g_hi
            glo_ref[pr, rows, :] = g_lo

    def advance(slot, states):
        rows = pl.ds(aligned(slot * L, L), L)
        split = [_split2(_bd(h)) for h in states]
        new = []
        for pr in range(pairs):
            h_hi, h_lo = split[pr]
            g_hi = ghi_ref[pr, rows, :]
            new.append(_dot(g_hi, h_hi) + _dot(glo_ref[pr, rows, :], h_hi) + _dot(g_hi, h_lo)
                       + hadd_ref[pr, rows, :])
        for pr in range(pairs):
            y0_ref[pr, rows, :] = _dot(rp_ref[pr, rows, :].astype(BF16), split[pr][0]) + y0_ref[pr, rows, :]
        return tuple(new)

    def finish(tiles, hooks):
        hooks = list(hooks)

        def tick():
            if hooks:
                hooks.pop(0)()

        rows = [pl.ds(t * NORM_ROWS, NORM_ROWS) for t, _ in tiles]
        y = [y0_ref[pr, pl.ds(t * NORM_ROWS + LAG * L, NORM_ROWS), :] for t, pr in tiles]
        mean = [_head_sum(x, ones_bd) * (1.0 / HEAD_DIM) for x in y]
        tick()
        yc = [x - m for x, m in zip(y, mean)]
        var = [_head_sum(x * x, ones_bd) * (1.0 / HEAD_DIM) for x in yc]
        tick()
        for (_, pr), rw, x, s in zip(tiles, rows, yc, var):
            prm = prm_ref[pr]
            yn = x * lax.rsqrt(s + GN_EPS) * prm[8:9] + prm[9:10] + bonus_ref[pr, rw, :]
            z = z_ref[pr, rw, :]
            o_ref[rw, pr * LANES:(pr + 1) * LANES] = (yn * (z / (1.0 + jnp.exp(-z)))).astype(o_ref.dtype)
        while hooks:
            tick()

    for pr in range(pairs):
        ghi_ref[pr, 0:LAG * L, :] = jnp.zeros((LAG * L, LANES), BF16)
        glo_ref[pr, 0:LAG * L, :] = jnp.zeros((LAG * L, LANES), BF16)
        for ref in (hadd_ref, rp_ref, y0_ref):
            ref[pr, 0:LAG * L, :] = jnp.zeros((LAG * L, LANES), F32)

    for _ in stage_a(0, 0):
        pass

    def body(i, states):
        states = list(states)
        parity = i & 1
        gen = stage_a(jnp.minimum(i + 1, n_groups - 1), 1 - parity)

        def step(u):
            def hook():
                states[:] = advance(i * LAG + u, tuple(states))
            return hook

        def next_a():
            next(gen, None)

        hooks = [[] for _ in range(B_TICKS)]
        for n in A_TICKS:
            hooks[n].append(next_a)
        for u in range(LAG):
            hooks[u * B_TICKS // LAG].append(step(u))
        stage_b(i, parity, hooks)
        for _ in gen:
            pass
        return tuple(states)

    states = lax.fori_loop(0, n_groups, body,
                           tuple(jnp.zeros((HEAD_DIM, LANES), F32) for _ in range(pairs)))

    states = list(states)

    def tail_step(u):
        def hook():
            states[:] = advance(n_chunks + u, tuple(states))
        return hook

    tail = [tail_step(u) for u in range(LAG)]
    n_tiles = seq // NORM_ROWS
    groups = [list(range(g, g + NORM_UNROLL)) for g in range(0, n_tiles, NORM_UNROLL)]
    n_early = sum(1 for g in groups if (g[-1] + 1) * NORM_ROWS <= seq - LAG * L)
    per_group = -(-LAG // max(n_early, 1))
    for g in groups:
        if (g[-1] + 1) * NORM_ROWS <= seq - LAG * L:
            mine, tail = tail[:per_group], tail[per_group:]
        else:
            while tail:
                tail.pop(0)()
            mine = []
        finish([(t, pr) for t in g for pr in range(pairs)], mine)


def _rwkv(p3, plora, prm, wup, *, batch, seq):
    pairs = RWKV_PAIRS_PER_STEP
    groups = RWKV_WIDTH // LANES // pairs
    n_chunks = seq // CHUNK
    assert n_chunks % LAG == 0 and n_chunks >= 2 * LAG
    assert seq % (NORM_ROWS * NORM_UNROLL) == 0

    def slab(base):
        return pl.BlockSpec((pairs, seq, LANES), lambda b, h: (base // pairs + h, b, 0))

    def per_pair(rows, dtype):
        return pltpu.VMEM((pairs, rows, LANES), dtype)

    return pl.pallas_call(
        functools.partial(_rwkv_kernel, seq=seq, pairs=pairs),
        grid=(batch, groups),
        in_specs=[
            slab(SLAB_R), slab(SLAB_K), slab(SLAB_V), slab(SLAB_ZA),
            pl.BlockSpec((seq, LANES), lambda b, h: (b, 0)),
            pl.BlockSpec((pairs, 16, LANES), lambda b, h: (h, 0, 0)),
            pl.BlockSpec((pairs, LANES, 2 * LANES), lambda b, h: (h, 0, 0)),
        ],
        out_specs=pl.BlockSpec((seq, pairs * LANES), lambda b, h: (b, h)),
        out_shape=jax.ShapeDtypeStruct((batch * seq, RWKV_WIDTH), BF16),
        scratch_shapes=[
            per_pair(seq + LAG * CHUNK, BF16), per_pair(seq + LAG * CHUNK, BF16),
            per_pair(seq + LAG * CHUNK, F32), per_pair(seq + LAG * CHUNK, F32),
            per_pair(seq + LAG * CHUNK, F32), per_pair(seq, F32),
            pltpu.VMEM((2, LAG * pairs, 6, CHUNK, LANES), BF16),
            pltpu.VMEM((2, LAG * pairs, CHUNK + 8, LANES), F32),
        ],
        compiler_params=pltpu.CompilerParams(
            dimension_semantics=("arbitrary", "arbitrary"), vmem_limit_bytes=VMEM_LIMIT),
        name="rwkv7_mix",
    )(p3, p3, p3, p3, plora, prm, wup)


def _moba_kernel(q_ref, k_ref, v_ref, z_ref, qw_ref, kw_ref, o_ref,
                 ks_ref, qt_ref, vt_ref, gate_ref, *, seq):
    blk = MOBA_BLOCK
    nb = seq // blk
    assert nb <= ONES_ROWS
    ones_bd = _head_ones()
    first = _lane_iota((seq, LANES)) < HEAD_DIM

    def normed(ref, w_ref):
        x = ref[...]
        ms = _head_sum(x * x, ones_bd) * (1.0 / HEAD_DIM)
        return x * lax.rsqrt(ms + RMS_EPS) * w_ref[...]

    qn = normed(q_ref, qw_ref)
    kn = normed(k_ref, kw_ref)
    qs = qn * (HEAD_DIM ** -0.5 * LOG2_E)
    eye = jnp.where(_lane_iota((LANES, LANES)) == _row_iota((LANES, LANES)), 1.0, 0.0).astype(BF16)
    block_of_key = lax.shift_right_logical(_row_iota((seq, LANES)), blk.bit_length() - 1)
    ks_ref[:, 0:LANES] = kn.astype(BF16)
    ks_ref[:, LANES:] = jnp.where(_lane_iota((seq, LANES)) == block_of_key, 1.0, 0.0).astype(BF16)
    for h in range(2):
        q_h = jnp.where(first if h == 0 else ~first, qs, 0.0).astype(BF16)
        qt_ref[h, 0:LANES, :] = _dot_nt(eye, q_h).astype(BF16)
        qt_ref[h, LANES:, :] = jnp.zeros((LANES, seq), BF16)
    v_t = _dot_nt(eye, v_ref[...].astype(BF16)).astype(BF16)
    for h in range(2):
        vt_ref[h, 0:HEAD_DIM, :] = v_t[h * HEAD_DIM:(h + 1) * HEAD_DIM, :]
        vt_ref[h, HEAD_DIM:, :] = jnp.ones((ONES_ROWS, seq), BF16)

    km = jnp.mean(kn.reshape(nb, blk, LANES), axis=1)
    first_nb = _lane_iota((nb, LANES)) < HEAD_DIM
    km_bd = jnp.concatenate([jnp.where(first_nb, km, 0.0), jnp.where(first_nb, 0.0, km)], axis=0)
    km_hi, km_lo = _split2(km_bd)
    q_hi, q_lo = _split2(qn)
    gate_ref[...] = _dot_nt(km_hi, q_hi) + _dot_nt(km_hi, q_lo) + _dot_nt(km_lo, q_hi)

    blk_row = _row_iota((nb, blk))
    mask_row = _row_iota((ONES_ROWS, blk))
    for i in range(MOBA_TOPK + 1, nb):
        for h in range(2):
            gate = gate_ref[h * nb:(h + 1) * nb, i * blk:(i + 1) * blk]
            rows = jnp.zeros((ONES_ROWS, blk), F32)
            for j in range(i):
                gj = gate[j:j + 1, :]
                beats = (blk_row < i) & ((gate > gj) | ((gate == gj) & (blk_row < j)))
                rank = jnp.sum(jnp.where(beats, 1.0, 0.0), axis=0, keepdims=True)
                rows = jnp.where((mask_row == j) & (rank >= MOBA_TOPK), NEG_INF, rows)
            qt_ref[h, LANES:LANES + ONES_ROWS, i * blk:(i + 1) * blk] = rows.astype(BF16)

    causal_t = _row_iota((blk, blk)) <= _lane_iota((blk, blk))

    def scores(i, h):
        return _dot(ks_ref[0:(i + 1) * blk, :], qt_ref[h, :, i * blk:(i + 1) * blk])

    def attend(i, h, s):
        own = jnp.where(causal_t, s[i * blk:, :], NEG_INF)
        s = jnp.concatenate([s[:i * blk, :], own], axis=0) if i > 0 else own
        m = jnp.max(s, axis=0, keepdims=True)
        e = jnp.exp2(s - m)
        pv = _dot(vt_ref[h, :, 0:(i + 1) * blk], e.astype(BF16))
        return pv[:HEAD_DIM] * (1.0 / pv[HEAD_DIM:HEAD_DIM + 1])

    units = [(i, h) for i in range(nb) for h in range(2)]
    ahead = [scores(*u) for u in units[:SCORE_LOOKAHEAD]]
    outs = []
    for n, (i, h) in enumerate(units):
        s_cur = ahead.pop(0)
        if n + SCORE_LOOKAHEAD < len(units):
            ahead.append(scores(*units[n + SCORE_LOOKAHEAD]))
        outs.append(attend(i, h, s_cur))
        if h == 1:
            rows = slice(i * blk, (i + 1) * blk)
            o = jnp.concatenate([outs[-2], outs[-1]], axis=0).T
            z = z_ref[rows, :]
            o_ref[rows, :] = (o * (z / (1.0 + jnp.exp(-z)))).astype(o_ref.dtype)


def _moba(p3, qw, kw, *, batch, seq):
    hp = MOBA_WIDTH // LANES

    def slab(base):
        return pl.BlockSpec((None, seq, LANES), lambda b, h: (base + h, b, 0))

    return pl.pallas_call(
        functools.partial(_moba_kernel, seq=seq),
        grid=(batch, hp),
        in_specs=[
            slab(SLAB_Q), slab(SLAB_KQ), slab(SLAB_VQ), slab(SLAB_ZB),
            pl.BlockSpec((1, LANES), lambda b, h: (0, 0)),
            pl.BlockSpec((1, LANES), lambda b, h: (0, 0)),
        ],
        out_specs=pl.BlockSpec((seq, LANES), lambda b, h: (b, h)),
        out_shape=jax.ShapeDtypeStruct((batch * seq, MOBA_WIDTH), BF16),
        scratch_shapes=[
            pltpu.VMEM((seq, 2 * LANES), BF16), pltpu.VMEM((2, 2 * LANES, seq), BF16),
            pltpu.VMEM((2, HEAD_DIM + ONES_ROWS, seq), BF16),
            pltpu.VMEM((2 * (seq // MOBA_BLOCK), seq), F32),
        ],
        compiler_params=pltpu.CompilerParams(
            dimension_semantics=("arbitrary", "arbitrary"), vmem_limit_bytes=VMEM_LIMIT),
        name="moba_attention",
    )(p3, p3, p3, p3, qw, kw)


def _out_kernel(x_ref, ya_ref, yb_ref, g_ref, wa_ref, wb_ref, wo_ref, o_ref, *, d_model):
    n = d_model // LANES
    pa = _dot(ya_ref[...], wa_ref[...])
    pb = _dot(yb_ref[...], wb_ref[...])
    ga = jnp.concatenate([g_ref[c] for c in range(n)], axis=1)
    gb = jnp.concatenate([g_ref[n + c] for c in range(n)], axis=1)
    merged = pa / (1.0 + jnp.exp(-ga)) + pb / (1.0 + jnp.exp(-gb))
    o_ref[...] = x_ref[...] + _dot(merged.astype(BF16), wo_ref[...])


def _out_proj(x2, ya, yb, p3, wa, wb, wo, *, tm):
    m, d = x2.shape
    n_g = 2 * d // LANES
    const = dict(pipeline_mode=pl.Buffered(1))
    return pl.pallas_call(
        functools.partial(_out_kernel, d_model=d),
        grid=(m // tm,),
        in_specs=[
            pl.BlockSpec((tm, d), lambda i: (i, 0)),
            pl.BlockSpec((tm, RWKV_WIDTH), lambda i: (i, 0)),
            pl.BlockSpec((tm, MOBA_WIDTH), lambda i: (i, 0)),
            pl.BlockSpec((n_g, tm, LANES), lambda i: (SLAB_G // n_g, i, 0)),
            pl.BlockSpec((RWKV_WIDTH, d), lambda i: (0, 0), **const),
            pl.BlockSpec((MOBA_WIDTH, d), lambda i: (0, 0), **const),
            pl.BlockSpec((d, d), lambda i: (0, 0), **const),
        ],
        out_specs=pl.BlockSpec((tm, d), lambda i: (i, 0)),
        out_shape=jax.ShapeDtypeStruct((m, d), F32),
        compiler_params=pltpu.CompilerParams(
            dimension_semantics=("arbitrary",), vmem_limit_bytes=VMEM_LIMIT),
        name="merge_out_proj",
    )(x2, ya, yb, p3, wa, wb, wo)


def _layer(x2, batch, seq, norm_w, w_in, mu_r, mu_k, mu_v, mu_w, mu_a, w0, w_decay_up, a0, w_iclr_up,
           k_k, k_a, r_k, gn_w, gn_b, q_norm_w, k_norm_w, w_proj_rwkv, w_proj_moba, w_out):
    d = x2.shape[1]
    assert d == (N_SLABS * LANES - SLAB_G * LANES) // 2
    p3, plora = _in_proj(x2, norm_w.reshape(1, d), w_in, tm=min(2048, x2.shape[0]), tn=512)

    hp = RWKV_WIDTH // LANES
    vecs = jnp.stack([mu_r, mu_k, mu_v, w0, a0, k_k, k_a, r_k, gn_w, gn_b]).reshape(10, hp, LANES)
    mu_l = jnp.broadcast_to(jnp.concatenate([mu_w, mu_a]).reshape(1, 1, LANES), (1, hp, LANES))
    prm = jnp.concatenate([vecs, mu_l, jnp.zeros((5, hp, LANES), F32)], axis=0).transpose(1, 0, 2)
    zeros = jnp.zeros((hp, LORA, LANES), F32)
    wd = w_decay_up.reshape(LORA, hp, LANES).transpose(1, 0, 2)
    wa_up = w_iclr_up.reshape(LORA, hp, LANES).transpose(1, 0, 2)
    wup = jnp.concatenate([jnp.concatenate([wd, zeros], axis=2),
                           jnp.concatenate([zeros, wa_up], axis=2)], axis=1)

    ya = _rwkv(p3, plora, prm, wup, batch=batch, seq=seq)
    qw = jnp.tile(q_norm_w, 2).reshape(1, LANES)
    kw = jnp.tile(k_norm_w, 2).reshape(1, LANES)
    yb = _moba(p3, qw, kw, batch=batch, seq=seq)
    return _out_proj(x2, ya, yb, p3, w_proj_rwkv.astype(BF16), w_proj_moba.astype(BF16),
                     w_out.astype(BF16), tm=256)


def kernel(x, norm_w, w_in, mu_r, mu_k, mu_v, mu_w, mu_a, w0, w_decay_up, a0, w_iclr_up, k_k, k_a, r_k,
           gn_w, gn_b, q_norm_w, k_norm_w, w_proj_rwkv, w_proj_moba, w_out):
    batch, seq, d = x.shape
    assert seq % MOBA_BLOCK == 0 and seq % CHUNK == 0
    params = (norm_w, w_in, mu_r, mu_k, mu_v, mu_w, mu_a, w0, w_decay_up, a0, w_iclr_up, k_k, k_a, r_k,
              gn_w, gn_b, q_norm_w, k_norm_w, w_proj_rwkv, w_proj_moba, w_out)
    x2 = x.reshape(batch * seq, d)
    for layer in range(norm_w.shape[0]):
        x2 = _layer(x2, batch, seq, *[p[layer] for p in params])
    return x2.reshape(batch, seq, d)
```
